```python
import jax, jax.numpy as jnp
from jax import lax
import numpy as np

D_MODEL = 2048
BATCH = 4
SEQ = 2048
DEPTH = 2

SB_HEADS = 8
SB_HEAD_DIM = 128
SB_WIDTH = SB_HEADS * SB_HEAD_DIM
SB_BLOCK = 128
RW_HEAD_DIM = 64
RW_WIDTH = D_MODEL - SB_WIDTH
RW_HEADS = RW_WIDTH // RW_HEAD_DIM
RW_DECAY_LORA = 64
RW_AAA_LORA = 64
RW_GATE_LORA = 160
RW_LORA = RW_DECAY_LORA + RW_AAA_LORA + RW_GATE_LORA
RW_SHIFTED = 3 * RW_WIDTH + RW_LORA
EVEN_IN = 3 * SB_WIDTH + RW_SHIFTED
RW_GN_EPS = 64e-5
GLA_HEADS = 4
GLA_KEY = D_MODEL // 2
GLA_VAL = D_MODEL
GLA_DK = GLA_KEY // GLA_HEADS
GLA_DV = GLA_VAL // GLA_HEADS
GLA_GATE_RANK = 16
GLA_TAU = 16.0
GLA_CHUNK = 64
ODD_IN = 2 * GLA_KEY + 2 * GLA_VAL + GLA_GATE_RANK
D_FF = 5632
CONV_W = 3
N_EVEN = (DEPTH + 1) // 2
N_ODD = DEPTH // 2
DN_ALPHA = (2 * DEPTH) ** 0.25
DN_BETA = (8 * DEPTH) ** -0.25
LN_EPS = 1e-5

kernel_name = "hybrid_stickbreak_rwkv7_gla_convffn_deepnorm"


def split_last(u, sizes):
    idx, acc = [], 0
    for s in sizes[:-1]:
        acc += s
        idx.append(acc)
    return jnp.split(u, idx, axis=-1)


def layer_norm(x, g, b):
    xf = x.astype(jnp.float32)
    mu = jnp.mean(xf, -1, keepdims=True)
    var = jnp.mean(jnp.square(xf - mu), -1, keepdims=True)
    y = (xf - mu) * lax.rsqrt(var + LN_EPS)
    return (y * g.astype(jnp.float32) + b.astype(jnp.float32)).astype(x.dtype)


def group_norm(x, g, b, n_groups, eps):
    shp = x.shape
    xf = x.astype(jnp.float32).reshape(shp[:-1] + (n_groups, shp[-1] // n_groups))
    mu = jnp.mean(xf, -1, keepdims=True)
    var = jnp.mean(jnp.square(xf - mu), -1, keepdims=True)
    y = ((xf - mu) * lax.rsqrt(var + eps)).reshape(shp)
    return y * g.astype(jnp.float32) + b.astype(jnp.float32)


def token_shift(u, mu):
    prev = jnp.pad(u, ((0, 0), (1, 0), (0, 0)))[:, :-1]
    return u + (prev - u) * mu


def stick_breaking_attention(q, k, v):
    B, H, S, dh = q.shape
    scale = dh ** -0.5
    vf = v.astype(jnp.float32)
    outs = []
    for blk in range(S // SB_BLOCK):
        q0 = blk * SB_BLOCK
        q1 = q0 + SB_BLOCK
        qb = q[:, :, q0:q1].astype(jnp.float32)
        kb = k[:, :, :q1].astype(jnp.float32)
        z = jnp.einsum('bhqd,bhkd->bhqk', qb, kb) * scale
        t_idx = jnp.arange(q0, q1)[:, None]
        s_idx = jnp.arange(q1)[None, :]
        mask = s_idx < t_idx
        log_keep = jnp.where(mask, jax.nn.log_sigmoid(-z), 0.0)
        cum = jnp.cumsum(log_keep, axis=-1)
        suffix = cum[..., -1:] - cum
        w = jnp.where(mask, jnp.exp(jax.nn.log_sigmoid(z) + suffix), 0.0)
        outs.append(jnp.einsum('bhqk,bhkd->bhqd', w, vf[:, :, :q1]))
    return jnp.concatenate(outs, axis=2)


def rwkv7_mix(r, k, v, dw, da, dg, w0, w2, a0, a2, g2, k_k, k_a, r_k, gn_g, gn_b):
    B, S, C = r.shape
    H, N = RW_HEADS, RW_HEAD_DIM
    f32 = jnp.float32
    r, k, v, dw, da, dg = (t.astype(f32) for t in (r, k, v, dw, da, dg))
    w_log = -jax.nn.softplus(-(w0 + jnp.tanh(dw) @ w2)) - 0.5
    decay = jnp.exp(-jnp.exp(w_log))
    a = jax.nn.sigmoid(a0 + da @ a2)
    g = jax.nn.sigmoid(dg) @ g2
    kk = (k * k_k).reshape(B, S, H, N)
    kk = kk * lax.rsqrt(jnp.maximum(jnp.sum(kk * kk, -1, keepdims=True), 1e-24))
    k = k * (1.0 + (a - 1.0) * k_a)
    rh, kh, vh, wh, ah = (t.reshape(B, S, H, N) for t in (r, k, v, decay, a))

    def step(state, inp):
        r_t, k_t, v_t, w_t, kk_t, a_t = inp
        sa = jnp.einsum('bhvk,bhk->bhv', state, -kk_t)
        state = (state * w_t[:, :, None, :]
                 + sa[..., None] * (kk_t * a_t)[:, :, None, :]
                 + v_t[..., None] * k_t[:, :, None, :])
        y = jnp.einsum('bhvk,bhk->bhv', state, r_t)
        return state, y

    xs = tuple(jnp.moveaxis(t, 1, 0) for t in (rh, kh, vh, wh, kk, ah))
    init = jnp.zeros((B, H, N, N), f32)
    _, y = lax.scan(step, init, xs)
    y = jnp.moveaxis(y, 0, 1).reshape(B, S, C)
    y = group_norm(y, gn_g, gn_b, H, RW_GN_EPS)
    bonus = jnp.sum(rh * kh * r_k.astype(f32), -1, keepdims=True) * vh
    return (y + bonus.reshape(B, S, C)) * g


def gla_chunked(q, k, v, log_a):
    B, S, H, dk = q.shape
    dv = v.shape[-1]
    L = GLA_CHUNK
    n = S // L
    f32 = jnp.float32

    def chunk(t):
        return t.astype(f32).reshape(B, n, L, H, t.shape[-1]).transpose(0, 3, 1, 2, 4)

    q, k, v, log_a = chunk(q) * dk ** -0.5, chunk(k), chunk(v), chunk(log_a)
    b = jnp.cumsum(log_a, axis=-2)
    b_last = b[..., -1:, :]
    q_dec = q * jnp.exp(b)
    k_inv = k * jnp.exp(-b)
    k_end = k * jnp.exp(b_last - b)
    causal = jnp.tril(jnp.ones((L, L), bool))
    att = jnp.where(causal, jnp.einsum('bhnid,bhnjd->bhnij', q_dec, k_inv), 0.0)
    o_intra = jnp.einsum('bhnij,bhnjv->bhniv', att, v)

    def step(state, inp):
        qc, kc, vc, dl = inp
        o = jnp.einsum('bhld,bhdv->bhlv', qc, state)
        state = state * dl[..., None] + jnp.einsum('bhld,bhlv->bhdv', kc, vc)
        return state, o

    xs = (jnp.moveaxis(q_dec, 2, 0), jnp.moveaxis(k_end, 2, 0), jnp.moveaxis(v, 2, 0),
          jnp.moveaxis(jnp.exp(b_last[..., 0, :]), 2, 0))
    _, o_inter = lax.scan(step, jnp.zeros((B, H, dk, dv), f32), xs)
    o = o_intra + jnp.moveaxis(o_inter, 0, 2)
    return o.transpose(0, 2, 3, 1, 4).reshape(B, S, H * dv)


def even_mixer(x, w_in, shift_mu, w0, w2, a0, a2, g2, k_k, k_a, r_k, gn_g, gn_b, w_out):
    B, S, _ = x.shape
    u = x @ w_in
    sb, rw = u[..., :3 * SB_WIDTH], u[..., 3 * SB_WIDTH:]
    q, k, v = (t.reshape(B, S, SB_HEADS, SB_HEAD_DIM).transpose(0, 2, 1, 3)
               for t in split_last(sb, [SB_WIDTH] * 3))
    o_sb = stick_breaking_attention(q, k, v).transpose(0, 2, 1, 3).reshape(B, S, SB_WIDTH)
    rw = token_shift(rw, shift_mu)
    r, kr, vr, dw, da, dg = split_last(
        rw, [RW_WIDTH] * 3 + [RW_DECAY_LORA, RW_AAA_LORA, RW_GATE_LORA])
    o_rw = rwkv7_mix(r, kr, vr, dw, da, dg, w0, w2, a0, a2, g2, k_k, k_a, r_k, gn_g, gn_b)
    o = jnp.concatenate([o_sb, o_rw], axis=-1).astype(x.dtype)
    return o @ w_out


def odd_mixer(x, w_in, gate_w2, gate_b, r_b, gn_g, gn_b, w_out):
    B, S, _ = x.shape
    u = x @ w_in
    q, k, v, dg, r = split_last(u, [GLA_KEY, GLA_KEY, GLA_VAL, GLA_GATE_RANK, GLA_VAL])
    log_a = jax.nn.log_sigmoid((dg.astype(jnp.float32) @ gate_w2 + gate_b)) / GLA_TAU
    hd = lambda t, d: t.reshape(B, S, GLA_HEADS, d)
    o = gla_chunked(hd(q, GLA_DK), hd(k, GLA_DK), hd(v, GLA_DV), hd(log_a, GLA_DK))
    o = group_norm(o, gn_g, gn_b, GLA_HEADS, LN_EPS)
    o = o * jax.nn.silu(r.astype(jnp.float32) + r_b)
    return o.astype(x.dtype) @ w_out


def conv_ffn(x, w_up, conv_w, conv_b, w_down):
    S = x.shape[1]
    gate, up = jnp.split(x @ w_up, 2, axis=-1)
    gate = lax.conv_general_dilated(
        gate, conv_w.reshape(CONV_W, 1, D_FF).astype(gate.dtype),
        window_strides=(1,), padding=[(CONV_W - 1, 0)],
        dimension_numbers=('NWC', 'WIO', 'NWC'), feature_group_count=D_FF) + conv_b
    h = jax.nn.gelu(gate, approximate=False) * up
    return h @ w_down


def setup_inputs(seed: int = 0) -> dict:
    key = jax.random.key(seed)
    ks = iter(jax.random.split(key, 48))

    def nrm(shape, scale):
        return jax.random.normal(next(ks), shape, jnp.float32) * scale

    D = D_MODEL
    sd = D ** -0.5
    x = nrm((BATCH, SEQ, D), 1.0)
    even_w_in = jnp.concatenate([
        nrm((N_EVEN, D, 2 * SB_WIDTH), sd),
        nrm((N_EVEN, D, SB_WIDTH), sd * DN_BETA),
        nrm((N_EVEN, D, 2 * RW_WIDTH), sd),
        nrm((N_EVEN, D, RW_WIDTH), sd * DN_BETA),
        nrm((N_EVEN, D, RW_LORA), sd),
    ], axis=-1)
    even_shift_mu = jax.random.uniform(next(ks), (N_EVEN, RW_SHIFTED), jnp.float32)
    rw_w0 = jax.random.uniform(next(ks), (N_EVEN, RW_WIDTH), jnp.float32, -3.0, 1.0)
    rw_w2 = nrm((N_EVEN, RW_DECAY_LORA, RW_WIDTH), 0.1 * RW_DECAY_LORA ** -0.5)
    rw_a0 = nrm((N_EVEN, RW_WIDTH), 0.1)
    rw_a2 = nrm((N_EVEN, RW_AAA_LORA, RW_WIDTH), 0.5 * RW_AAA_LORA ** -0.5)
    rw_g2 = nrm((N_EVEN, RW_GATE_LORA, RW_WIDTH), RW_GATE_LORA ** -0.5)
    rw_k_k = 0.85 + nrm((N_EVEN, RW_WIDTH), 0.02)
    rw_k_a = 1.0 + nrm((N_EVEN, RW_WIDTH), 0.02)
    rw_r_k = nrm((N_EVEN, RW_HEADS, RW_HEAD_DIM), 0.1)
    rw_gn_g = 1.0 + nrm((N_EVEN, RW_WIDTH), 0.02)
    rw_gn_b = nrm((N_EVEN, RW_WIDTH), 0.02)
    even_w_out = nrm((N_EVEN, D, D), sd * DN_BETA)
    odd_w_in = jnp.concatenate([
        nrm((N_ODD, D, 2 * GLA_KEY), sd),
        nrm((N_ODD, D, GLA_VAL), sd * DN_BETA),
        nrm((N_ODD, D, GLA_GATE_RANK), sd),
        nrm((N_ODD, D, GLA_VAL), sd),
    ], axis=-1)
    gla_gate_w2 = nrm((N_ODD, GLA_GATE_RANK, GLA_KEY), GLA_GATE_RANK ** -0.5)
    gla_gate_b = nrm((N_ODD, GLA_KEY), 0.1)
    gla_r_b = nrm((N_ODD, GLA_VAL), 0.02)
    gla_gn_g = 1.0 + nrm((N_ODD, GLA_VAL), 0.02)
    gla_gn_b = nrm((N_ODD, GLA_VAL), 0.02)
    odd_w_out = nrm((N_ODD, GLA_VAL, D), GLA_VAL ** -0.5 * DN_BETA)
    ln_mix_g = 1.0 + nrm((DEPTH, D), 0.02)
    ln_mix_b = nrm((DEPTH, D), 0.02)
    ln_ffn_g = 1.0 + nrm((DEPTH, D), 0.02)
    ln_ffn_b = nrm((DEPTH, D), 0.02)
    ffn_w_up = nrm((DEPTH, D, 2 * D_FF), sd * DN_BETA)
    ffn_conv_w = nrm((DEPTH, CONV_W, D_FF), CONV_W ** -0.5)
    ffn_conv_b = nrm((DEPTH, D_FF), 0.02)
    ffn_w_down = nrm((DEPTH, D_FF, D), D_FF ** -0.5 * DN_BETA)
    return {"x": x, "even_w_in": even_w_in, "even_shift_mu": even_shift_mu,
            "rw_w0": rw_w0, "rw_w2": rw_w2, "rw_a0": rw_a0, "rw_a2": rw_a2,
            "rw_g2": rw_g2, "rw_k_k": rw_k_k, "rw_k_a": rw_k_a, "rw_r_k": rw_r_k,
            "rw_gn_g": rw_gn_g, "rw_gn_b": rw_gn_b, "even_w_out": even_w_out,
            "odd_w_in": odd_w_in, "gla_gate_w2": gla_gate_w2, "gla_gate_b": gla_gate_b,
            "gla_r_b": gla_r_b, "gla_gn_g": gla_gn_g, "gla_gn_b": gla_gn_b,
            "odd_w_out": odd_w_out, "ln_mix_g": ln_mix_g, "ln_mix_b": ln_mix_b,
            "ln_ffn_g": ln_ffn_g, "ln_ffn_b": ln_ffn_b, "ffn_w_up": ffn_w_up,
            "ffn_conv_w": ffn_conv_w, "ffn_conv_b": ffn_conv_b, "ffn_w_down": ffn_w_down}


def reference(x, even_w_in, even_shift_mu, rw_w0, rw_w2, rw_a0, rw_a2, rw_g2, rw_k_k,
              rw_k_a, rw_r_k, rw_gn_g, rw_gn_b, even_w_out, odd_w_in, gla_gate_w2,
              gla_gate_b, gla_r_b, gla_gn_g, gla_gn_b, odd_w_out, ln_mix_g, ln_mix_b,
              ln_ffn_g, ln_ffn_b, ffn_w_up, ffn_conv_w, ffn_conv_b, ffn_w_down):
    h = x
    for layer in range(DEPTH):
        i = layer // 2
        if layer % 2 == 0:
            m = even_mixer(h, even_w_in[i], even_shift_mu[i], rw_w0[i], rw_w2[i], rw_a0[i],
                           rw_a2[i], rw_g2[i], rw_k_k[i], rw_k_a[i], rw_r_k[i],
                           rw_gn_g[i], rw_gn_b[i], even_w_out[i])
        else:
            m = odd_mixer(h, odd_w_in[i], gla_gate_w2[i], gla_gate_b[i], gla_r_b[i],
                          gla_gn_g[i], gla_gn_b[i], odd_w_out[i])
        h = layer_norm(DN_ALPHA * h + m.astype(h.dtype), ln_mix_g[layer], ln_mix_b[layer])
        f = conv_ffn(h, ffn_w_up[layer], ffn_conv_w[layer], ffn_conv_b[layer], ffn_w_down[layer])
        h = layer_norm(DN_ALPHA * h + f.astype(h.dtype), ln_ffn_g[layer], ln_ffn_b[layer])
    return h
```

```python
import functools
import math

import jax
import jax.numpy as jnp
from jax import lax
from jax.experimental import pallas as pl
from jax.experimental.pallas import tpu as pltpu

F32 = jnp.float32
BF16 = jnp.bfloat16
HIGHEST = lax.Precision.HIGHEST

SB_HEADS = 8
SB_HEAD_DIM = 128
RW_HEAD_DIM = 64
RW_DECAY_LORA = 64
RW_AAA_LORA = 64
RW_GATE_LORA = 160
RW_GN_EPS = 64e-5
GLA_HEADS = 4
GLA_GATE_RANK = 16
GLA_TAU = 16.0
CONV_W = 3
LN_EPS = 1e-5

LANES = 128
SUBLANES = 8
VMEM_LIMIT_BYTES = 56 * 1024 * 1024

MM_TM = 512
MM_TN = 512
LN_TM = 512
FFN_TM = 512
FFN_TN = 512
DOWN_TK = 1408
SB_TQ = 256
RW_CHUNK = 64
GLA_CHUNK = 64
GLA_TB = 256


def _round_up(n, m):
    return (n + m - 1) // m * m


def _params(*sem):
    return pltpu.CompilerParams(dimension_semantics=sem, vmem_limit_bytes=VMEM_LIMIT_BYTES)


def _log_sigmoid(x):
    return jnp.minimum(x, 0.0) - jnp.log1p(jnp.exp(-jnp.abs(x)))


def _sigmoid(x):
    return 1.0 / (1.0 + jnp.exp(-x))


def _dot(a, b):
    return jnp.dot(a.astype(BF16), b.astype(BF16), preferred_element_type=F32)


def _dot_nt(a, b):
    return lax.dot_general(a.astype(BF16), b.astype(BF16), (((1,), (1,)), ((), ())),
                           preferred_element_type=F32)


def _dot_tn(a, b):
    return lax.dot_general(a.astype(BF16), b.astype(BF16), (((0,), (0,)), ((), ())),
                           preferred_element_type=F32)


def _dot_exact(a, b):
    return jnp.dot(a, b, preferred_element_type=F32, precision=HIGHEST)


def _layer_norm(y, g, b, eps):
    mu = jnp.mean(y, axis=-1, keepdims=True)
    d = y - mu
    var = jnp.mean(d * d, axis=-1, keepdims=True)
    return d * lax.rsqrt(var + eps) * g + b


def _mm_kernel(a_ref, w_ref, o_ref):
    o_ref[...] = jnp.dot(a_ref[...], w_ref[...], preferred_element_type=F32).astype(o_ref.dtype)


def _matmul(a, w, out_dtype, name):
    m, k = a.shape
    n = w.shape[1]
    tm, tn = MM_TM, MM_TN
    assert m % tm == 0 and n % tn == 0
    return pl.pallas_call(
        _mm_kernel,
        grid=(n // tn, m // tm),
        in_specs=[pl.BlockSpec((tm, k), lambda j, i: (i, 0)),
                  pl.BlockSpec((k, tn), lambda j, i: (0, j))],
        out_specs=pl.BlockSpec((tm, tn), lambda j, i: (i, j)),
        out_shape=jax.ShapeDtypeStruct((m, n), out_dtype),
        compiler_params=_params("arbitrary", "arbitrary"),
        name=name,
    )(a, w)


def _proj_ln_kernel(*refs, n_parts, alpha):
    a_refs = refs[:n_parts]
    w_ref, res_ref, g_ref, b_ref, o_ref, obf_ref = refs[n_parts:]
    acc = None
    k0 = 0
    for a_ref in a_refs:
        kp = a_ref.shape[1]
        part = jnp.dot(a_ref[...], w_ref[k0:k0 + kp, :], preferred_element_type=F32)
        acc = part if acc is None else acc + part
        k0 += kp
    y = alpha * res_ref[...] + acc
    out = _layer_norm(y, g_ref[...], b_ref[...], LN_EPS)
    o_ref[...] = out
    obf_ref[...] = out.astype(BF16)


def _proj_ln(a_parts, w, res, g, b, alpha, name):
    m, d = res.shape
    k = w.shape[0]
    tm = LN_TM
    assert m % tm == 0 and sum(a.shape[1] for a in a_parts) == k
    in_specs = [pl.BlockSpec((tm, a.shape[1]), lambda i: (i, 0)) for a in a_parts]
    in_specs += [pl.BlockSpec((k, d), lambda i: (0, 0)),
                 pl.BlockSpec((tm, d), lambda i: (i, 0)),
                 pl.BlockSpec((1, d), lambda i: (0, 0)),
                 pl.BlockSpec((1, d), lambda i: (0, 0))]
    return pl.pallas_call(
        functools.partial(_proj_ln_kernel, n_parts=len(a_parts), alpha=alpha),
        grid=(m // tm,),
        in_specs=in_specs,
        out_specs=[pl.BlockSpec((tm, d), lambda i: (i, 0)),
                   pl.BlockSpec((tm, d), lambda i: (i, 0))],
        out_shape=[jax.ShapeDtypeStruct((m, d), F32), jax.ShapeDtypeStruct((m, d), BF16)],
        compiler_params=_params("arbitrary"),
        name=name,
    )(*a_parts, w, res, g.reshape(1, d), b.reshape(1, d))


def _sb_kernel(q_ref, k_ref, v_ref, o_ref, *, tq, scale):
    i = pl.program_id(2)
    q = q_ref[...]
    row = lax.broadcasted_iota(jnp.int32, (tq, tq), 0)
    col = lax.broadcasted_iota(jnp.int32, (tq, tq), 1)
    after = jnp.where(row > col, 1.0, 0.0).astype(BF16)
    causal = col < row

    def tile(j, carry, acc, diagonal):
        k0 = pl.multiple_of(j * tq, tq)
        kt = k_ref[pl.ds(k0, tq), :]
        vt = v_ref[pl.ds(k0, tq), :]
        z = lax.dot_general(q, kt, (((1,), (1,)), ((), ())), preferred_element_type=F32) * scale
        ls = _log_sigmoid(z)
        lk = ls - z
        if diagonal:
            lk = jnp.where(causal, lk, 0.0)
        hi = lk.astype(BF16)
        lo = (lk - hi.astype(F32)).astype(BF16)
        suffix = (jnp.dot(hi, after, preferred_element_type=F32)
                  + jnp.dot(lo, after, preferred_element_type=F32)) + carry
        w = jnp.exp(ls + suffix)
        if diagonal:
            w = jnp.where(causal, w, 0.0)
        acc = acc + jnp.dot(w.astype(BF16), vt, preferred_element_type=F32)
        carry = suffix[:, 0:1] + lk[:, 0:1]
        return carry, acc

    carry0 = jnp.zeros((tq, 1), F32)
    acc0 = jnp.zeros((tq, q.shape[1]), F32)
    carry, acc = tile(i, carry0, acc0, True)

    def body(n, state):
        return tile(i - 1 - n, state[0], state[1], False)

    carry, acc = lax.fori_loop(0, i, body, (carry, acc))
    o_ref[...] = acc.astype(o_ref.dtype)


def _sb_attention(u_sb, batch, seq):
    tq = SB_TQ
    nq = seq // tq
    dh = SB_HEAD_DIM
    h = SB_HEADS
    assert seq % tq == 0 and dh % LANES == 0
    return pl.pallas_call(
        functools.partial(_sb_kernel, tq=tq, scale=dh ** -0.5),
        grid=(batch, h, nq),
        in_specs=[pl.BlockSpec((tq, dh), lambda b, hh, i: (b * nq + i, hh)),
                  pl.BlockSpec((seq, dh), lambda b, hh, i: (b, h + hh)),
                  pl.BlockSpec((seq, dh), lambda b, hh, i: (b, 2 * h + hh))],
        out_specs=pl.BlockSpec((tq, dh), lambda b, hh, i: (b * nq + i, hh)),
        out_shape=jax.ShapeDtypeStruct((batch * seq, h * dh), BF16),
        compiler_params=_params("arbitrary", "arbitrary", "arbitrary"),
        name="sb_attention",
    )(u_sb, u_sb, u_sb)


def _rwkv_kernel(u_ref, mu_ref, w0_ref, w2_ref, a0_ref, a2_ref, g2_ref, kk_ref, ka_ref, rk_ref,
                 gng_ref, gnb_ref, o_ref,
                 state_ref, prev_ref, kkf_s, p1_s, p2_s, p3_s, kt_s, ke_s, rt_s, v_s, rkr_s, y_s,
                 *, chunk, width, n_heads):
    L, C = chunk, width
    N = C // n_heads
    c = pl.program_id(1)

    @pl.when(c == 0)
    def _():
        state_ref[...] = jnp.zeros_like(state_ref)
        prev_ref[...] = jnp.zeros_like(prev_ref)

    u = u_ref[...]
    row1 = lax.broadcasted_iota(jnp.int32, (L, 1), 0)
    prev = jnp.where(row1 == 0, prev_ref[SUBLANES - 1:SUBLANES, :], pltpu.roll(u, 1, axis=0))
    prev_ref[...] = u[L - SUBLANES:, :]
    us = u + (prev - u) * mu_ref[...]

    r = us[:, 0:C]
    k = us[:, C:2 * C]
    v = us[:, 2 * C:3 * C]
    o1 = 3 * C
    o2 = o1 + _round_up(RW_DECAY_LORA, LANES)
    o3 = o2 + _round_up(RW_AAA_LORA, LANES)
    dw = us[:, o1:o2]
    da = us[:, o2:o3]
    dg = us[:, o3:]

    w_log = _log_sigmoid(w0_ref[...] + _dot(jnp.tanh(dw), w2_ref[...])) - 0.5
    log_decay = -jnp.exp(w_log)
    a = _sigmoid(a0_ref[...] + _dot(da, a2_ref[...]))
    gate = _dot(_sigmoid(dg), g2_ref[...])

    row = lax.broadcasted_iota(jnp.int32, (L, L), 0)
    col = lax.broadcasted_iota(jnp.int32, (L, L), 1)
    strict = row > col
    incl = row >= col
    lc = _dot_exact(jnp.where(incl, 1.0, 0.0), log_decay)
    g_incl = jnp.exp(lc)
    g_excl = jnp.exp(lc - log_decay)
    g_inv = jnp.exp(-lc)
    g_last = g_incl[L - 1:L, :]

    kkf = k * kk_ref[...]
    kmod = k * (1.0 + (a - 1.0) * ka_ref[...])
    kkf_s[...] = kkf
    p1_s[...] = kkf * g_excl
    p2 = kkf * a * g_inv
    p2_s[...] = p2
    p3_s[...] = p2 * g_last
    kt = kmod * g_inv
    kt_s[...] = kt
    ke_s[...] = kt * g_last
    rt_s[...] = r * g_incl
    v_s[...] = v
    rkr_s[...] = r * kmod * rk_ref[...]

    eye = jnp.where(row == col, 1.0, 0.0)
    for h in range(n_heads):
        sl = slice(h * N, (h + 1) * N)
        kkf_h = kkf_s[:, sl]
        rs = lax.rsqrt(jnp.maximum(jnp.sum(kkf_h * kkf_h, axis=-1, keepdims=True), 1e-24))
        at = -(p1_s[:, sl] * rs)
        bt = p2_s[:, sl] * rs
        be = p3_s[:, sl] * rs
        kt_h = kt_s[:, sl]
        ke_h = ke_s[:, sl]
        rt_h = rt_s[:, sl]
        v_h = v_s[:, sl]
        s0 = state_ref[h]

        n_ab = jnp.where(strict, _dot_nt(at, bt), 0.0)
        a_ak = jnp.where(strict, _dot_nt(at, kt_h), 0.0)
        a_rb = jnp.where(incl, _dot_nt(rt_h, bt), 0.0)
        a_rk = jnp.where(incl, _dot_nt(rt_h, kt_h), 0.0)
        inv = eye + n_ab
        npow = n_ab
        span = 1
        while 2 * span < L:
            npow = _dot(npow, npow)
            inv = inv + _dot(inv, npow)
            span *= 2
        rhs = _dot_nt(at, s0) + _dot(a_ak, v_h)
        uu = _dot(inv, rhs)
        y = _dot_nt(rt_h, s0) + _dot(a_rb, uu) + _dot(a_rk, v_h)
        state_ref[h] = s0 * g_last[:, sl] + _dot_tn(uu, be) + _dot_tn(v_h, ke_h)

        mean = jnp.mean(y, axis=-1, keepdims=True)
        d = y - mean
        var = jnp.mean(d * d, axis=-1, keepdims=True)
        bonus = jnp.sum(rkr_s[:, sl], axis=-1, keepdims=True) * v_h
        y_s[:, sl] = (d * lax.rsqrt(var + RW_GN_EPS) * gng_ref[:, sl] + gnb_ref[:, sl]) + bonus

    o_ref[...] = (y_s[...] * gate).astype(o_ref.dtype)


def _rwkv(u_rw, batch, seq, mu, w0, w2, a0, a2, g2, k_k, k_a, r_k, gn_g, gn_b):
    L = RW_CHUNK
    width = w0.shape[0]
    n_heads = width // RW_HEAD_DIM
    n = RW_HEAD_DIM
    nc = seq // L
    wu = u_rw.shape[1]
    assert seq % L == 0 and L % SUBLANES == 0
    row = lambda x: x.reshape(1, -1)
    const = lambda shape: pl.BlockSpec(shape, lambda b, c: (0, 0))
    slab = pltpu.VMEM((L, width), F32)
    return pl.pallas_call(
        functools.partial(_rwkv_kernel, chunk=L, width=width, n_heads=n_heads),
        grid=(batch, nc),
        in_specs=[pl.BlockSpec((L, wu), lambda b, c: (b * nc + c, 0)),
                  const((1, wu)), const((1, width)), const(w2.shape), const((1, width)),
                  const(a2.shape), const(g2.shape), const((1, width)), const((1, width)),
                  const((1, width)), const((1, width)), const((1, width))],
        out_specs=pl.BlockSpec((L, width), lambda b, c: (b * nc + c, 0)),
        out_shape=jax.ShapeDtypeStruct((batch * seq, width), BF16),
        scratch_shapes=[pltpu.VMEM((n_heads, n, n), F32), pltpu.VMEM((SUBLANES, wu), F32)]
                       + [slab] * 10,
        compiler_params=_params("arbitrary", "arbitrary"),
        name="rwkv7",
    )(u_rw, row(mu), row(w0), w2, row(a0), a2, g2, row(k_k), row(k_a), row(r_k), row(gn_g),
      row(gn_b))


def _gla_kernel(q_ref, k_ref, v_ref, r_ref, dg_ref, gw_ref, gb_ref, rb_ref, gng_ref, gnb_ref,
                o_ref, state_ref, *, chunk, scale):
    L = chunk
    tb = q_ref.shape[0]
    t = pl.program_id(2)

    @pl.when(t == 0)
    def _():
        state_ref[...] = jnp.zeros_like(state_ref)

    log_a = _log_sigmoid(_dot_exact(dg_ref[...], gw_ref[...]) + gb_ref[...]) / GLA_TAU
    row = lax.broadcasted_iota(jnp.int32, (L, L), 0)
    col = lax.broadcasted_iota(jnp.int32, (L, L), 1)
    incl = row >= col
    lower = jnp.where(incl, 1.0, 0.0)
    for cc in range(tb // L):
        rows = slice(cc * L, (cc + 1) * L)
        b = _dot_exact(lower, log_a[rows, :])
        b_last = b[L - 1:L, :]
        q = q_ref[rows, :] * scale
        k = k_ref[rows, :]
        v = v_ref[rows, :]
        q_dec = q * jnp.exp(b)
        k_inv = k * jnp.exp(-b)
        k_end = k * jnp.exp(b_last - b)
        att = jnp.where(incl, _dot_nt(q_dec, k_inv), 0.0)
        st = state_ref[...]
        o = _dot(att, v) + _dot_nt(q_dec, st)
        state_ref[...] = st * jnp.exp(b_last) + _dot_tn(v, k_end)
        o = _layer_norm(o, gng_ref[...], gnb_ref[...], LN_EPS)
        x = r_ref[rows, :] + rb_ref[...]
        o_ref[rows, :] = (o * (x * _sigmoid(x))).astype(o_ref.dtype)


def _gla(u, dg, batch, seq, gate_w2, gate_b, r_b, gn_g, gn_b):
    tb = GLA_TB
    h = GLA_HEADS
    key = gate_b.shape[0]
    val = r_b.shape[0]
    dk, dv = key // h, val // h
    nt = seq // tb
    rank = dg.shape[1]
    assert seq % tb == 0 and tb % GLA_CHUNK == 0 and key % dv == 0
    kv = key // dv
    row = lambda x: x.reshape(1, -1)
    return pl.pallas_call(
        functools.partial(_gla_kernel, chunk=GLA_CHUNK, scale=dk ** -0.5),
        grid=(batch, h, nt),
        in_specs=[pl.BlockSpec((tb, dk), lambda b, hh, t: (b * nt + t, hh)),
                  pl.BlockSpec((tb, dk), lambda b, hh, t: (b * nt + t, h + hh)),
                  pl.BlockSpec((tb, dv), lambda b, hh, t: (b * nt + t, 2 * kv + hh)),
                  pl.BlockSpec((tb, dv), lambda b, hh, t: (b * nt + t, 2 * kv + h + hh)),
                  pl.BlockSpec((tb, rank), lambda b, hh, t: (b * nt + t, 0)),
                  pl.BlockSpec((rank, dk), lambda b, hh, t: (0, hh)),
                  pl.BlockSpec((1, dk), lambda b, hh, t: (0, hh)),
                  pl.BlockSpec((1, dv), lambda b, hh, t: (0, hh)),
                  pl.BlockSpec((1, dv), lambda b, hh, t: (0, hh)),
                  pl.BlockSpec((1, dv), lambda b, hh, t: (0, hh))],
        out_specs=pl.BlockSpec((tb, dv), lambda b, hh, t: (b * nt + t, hh)),
        out_shape=jax.ShapeDtypeStruct((batch * seq, val), BF16),
        scratch_shapes=[pltpu.VMEM((dv, dk), F32)],
        compiler_params=_params("arbitrary", "arbitrary", "arbitrary"),
        name="gla",
    )(u, u, u, u, dg, gate_w2, row(gate_b), row(r_b), row(gn_g), row(gn_b))


def _ffn_up_kernel(a_ref, wg_ref, wu_ref, cw_ref, cb_ref, o_ref, carry_ref, *, seq):
    i = pl.program_id(1)
    tm = a_ref.shape[0]
    a = a_ref[...]
    gate = jnp.dot(a, wg_ref[...], preferred_element_type=F32)
    up = jnp.dot(a, wu_ref[...], preferred_element_type=F32)
    first = (i * tm) % seq == 0
    tail = carry_ref[...]
    carry_ref[...] = gate[tm - SUBLANES:, :]
    p1 = jnp.where(first, 0.0, tail[SUBLANES - 1:SUBLANES, :])
    p2 = jnp.where(first, 0.0, tail[SUBLANES - 2:SUBLANES - 1, :])
    row = lax.broadcasted_iota(jnp.int32, (tm, 1), 0)
    g1 = jnp.where(row == 0, p1, pltpu.roll(gate, 1, axis=0))
    g2 = jnp.where(row == 0, p2, jnp.where(row == 1, p1, pltpu.roll(gate, 2, axis=0)))
    conv = cw_ref[0:1, :] * g2 + cw_ref[1:2, :] * g1 + cw_ref[2:3, :] * gate + cb_ref[...]
    act = 0.5 * conv * (1.0 + lax.erf(conv * math.sqrt(0.5)))
    o_ref[...] = (act * up).astype(o_ref.dtype)


def _ffn_up(a, w_up, conv_w, conv_b, seq):
    m, k = a.shape
    dff = conv_b.shape[0]
    tm, tn = FFN_TM, FFN_TN
    nj = dff // tn
    assert m % tm == 0 and dff % tn == 0 and seq % tm == 0 and conv_w.shape[0] == CONV_W
    return pl.pallas_call(
        functools.partial(_ffn_up_kernel, seq=seq),
        grid=(nj, m // tm),
        in_specs=[pl.BlockSpec((tm, k), lambda j, i: (i, 0)),
                  pl.BlockSpec((k, tn), lambda j, i: (0, j)),
                  pl.BlockSpec((k, tn), lambda j, i: (0, nj + j)),
                  pl.BlockSpec((CONV_W, tn), lambda j, i: (0, j)),
                  pl.BlockSpec((1, tn), lambda j, i: (0, j))],
        out_specs=pl.BlockSpec((tm, tn), lambda j, i: (i, j)),
        out_shape=jax.ShapeDtypeStruct((m, dff), BF16),
        scratch_shapes=[pltpu.VMEM((SUBLANES, tn), F32)],
        compiler_params=_params("arbitrary", "arbitrary"),
        name="ffn_up",
    )(a, w_up, w_up, conv_w, conv_b.reshape(1, dff))


def _ffn_down_kernel(a_ref, w_ref, res_ref, g_ref, b_ref, o_ref, obf_ref, acc_ref, *, alpha):
    kk = pl.program_id(1)
    part = jnp.dot(a_ref[...], w_ref[...], preferred_element_type=F32)

    @pl.when(kk == 0)
    def _():
        acc_ref[...] = part

    @pl.when(kk > 0)
    def _():
        acc_ref[...] += part

    @pl.when(kk == pl.num_programs(1) - 1)
    def _():
        y = alpha * res_ref[...] + acc_ref[...]
        out = _layer_norm(y, g_ref[...], b_ref[...], LN_EPS)
        o_ref[...] = out
        obf_ref[...] = out.astype(BF16)


def _ffn_down(a, w, res, g, b, alpha):
    m, d = res.shape
    k = w.shape[0]
    tm, tk = LN_TM, DOWN_TK
    assert m % tm == 0 and k % tk == 0
    return pl.pallas_call(
        functools.partial(_ffn_down_kernel, alpha=alpha),
        grid=(m // tm, k // tk),
        in_specs=[pl.BlockSpec((tm, tk), lambda i, kk: (i, kk)),
                  pl.BlockSpec((tk, d), lambda i, kk: (kk, 0)),
                  pl.BlockSpec((tm, d), lambda i, kk: (i, 0)),
                  pl.BlockSpec((1, d), lambda i, kk: (0, 0)),
                  pl.BlockSpec((1, d), lambda i, kk: (0, 0))],
        out_specs=[pl.BlockSpec((tm, d), lambda i, kk: (i, 0)),
                   pl.BlockSpec((tm, d), lambda i, kk: (i, 0))],
        out_shape=[jax.ShapeDtypeStruct((m, d), F32), jax.ShapeDtypeStruct((m, d), BF16)],
        scratch_shapes=[pltpu.VMEM((tm, d), F32)],
        compiler_params=_params("arbitrary", "arbitrary"),
        name="ffn_down",
    )(a, w, res, g.reshape(1, d), b.reshape(1, d))


def _pad_cols(w, n):
    return jnp.pad(w, ((0, 0), (0, n - w.shape[1])))


def _pad_rows(w, n):
    return jnp.pad(w, ((0, n - w.shape[0]), (0, 0)))


def _pad_vec(w, n):
    return jnp.pad(w, (0, n - w.shape[0]))


def kernel(x, even_w_in, even_shift_mu, rw_w0, rw_w2, rw_a0, rw_a2, rw_g2, rw_k_k, rw_k_a, rw_r_k, rw_gn_g, rw_gn_b, even_w_out, odd_w_in, gla_gate_w2, gla_gate_b, gla_r_b, gla_gn_g, gla_gn_b, odd_w_out, ln_mix_g, ln_mix_b, ln_ffn_g, ln_ffn_b, ffn_w_up, ffn_conv_w, ffn_conv_b, ffn_w_down):
    batch, seq, d = x.shape
    depth = ln_mix_g.shape[0]
    alpha = (2 * depth) ** 0.25
    m = batch * seq
    h = x.reshape(m, d)
    h_bf = h.astype(BF16)

    sb_w = SB_HEADS * SB_HEAD_DIM
    rw_w = rw_w0.shape[1]
    lw, la, lg = (_round_up(n, LANES) for n in (RW_DECAY_LORA, RW_AAA_LORA, RW_GATE_LORA))

    for layer in range(depth):
        i = layer // 2
        if layer % 2 == 0:
            w_in = even_w_in[i]
            c0 = 3 * sb_w
            c1 = c0 + 3 * rw_w
            c2 = c1 + RW_DECAY_LORA
            c3 = c2 + RW_AAA_LORA
            w_rw = jnp.concatenate([w_in[:, c0:c1], _pad_cols(w_in[:, c1:c2], lw),
                                    _pad_cols(w_in[:, c2:c3], la), _pad_cols(w_in[:, c3:], lg)],
                                   axis=1)
            mu = even_shift_mu[i]
            m0 = 3 * rw_w
            m1 = m0 + RW_DECAY_LORA
            m2 = m1 + RW_AAA_LORA
            mu_p = jnp.concatenate([mu[:m0], _pad_vec(mu[m0:m1], lw), _pad_vec(mu[m1:m2], la),
                                    _pad_vec(mu[m2:], lg)])
            u_sb = _matmul(h_bf, w_in[:, :c0].astype(BF16), BF16, "even_in_sb")
            u_rw = _matmul(h_bf, w_rw.astype(BF16), F32, "even_in_rw")
            o_sb = _sb_attention(u_sb, batch, seq)
            o_rw = _rwkv(u_rw, batch, seq, mu_p, rw_w0[i],
                         _pad_rows(rw_w2[i], lw).astype(BF16), rw_a0[i],
                         _pad_rows(rw_a2[i], la).astype(BF16),
                         _pad_rows(rw_g2[i], lg).astype(BF16),
                         rw_k_k[i], rw_k_a[i], rw_r_k[i].reshape(-1), rw_gn_g[i], rw_gn_b[i])
            h, h_bf = _proj_ln([o_sb, o_rw], even_w_out[i].astype(BF16), h, ln_mix_g[layer],
                               ln_mix_b[layer], alpha, "even_out_ln")
        else:
            w_in = odd_w_in[i]
            key = gla_gate_b.shape[1]
            val = gla_r_b.shape[1]
            c0 = 2 * key + val
            c1 = c0 + GLA_GATE_RANK
            rank_p = _round_up(GLA_GATE_RANK, LANES)
            w_main = jnp.concatenate([w_in[:, :c0], w_in[:, c1:]], axis=1)
            w_dg = _pad_cols(w_in[:, c0:c1], MM_TN)
            u = _matmul(h_bf, w_main.astype(BF16), F32, "odd_in")
            dg = _matmul(h_bf, w_dg.astype(BF16), F32, "odd_in_gate")[:, :rank_p]
            o = _gla(u, dg, batch, seq, _pad_rows(gla_gate_w2[i], rank_p), gla_gate_b[i],
                     gla_r_b[i], gla_gn_g[i], gla_gn_b[i])
            h, h_bf = _proj_ln([o], odd_w_out[i].astype(BF16), h, ln_mix_g[layer],
                               ln_mix_b[layer], alpha, "odd_out_ln")
        mid = _ffn_up(h_bf, ffn_w_up[layer].astype(BF16), ffn_conv_w[layer], ffn_conv_b[layer],
                      seq)
        h, h_bf = _ffn_down(mid, ffn_w_down[layer].astype(BF16), h, ln_ffn_g[layer],
                            ln_ffn_b[layer], alpha)
    return h.reshape(batch, seq, d)
```

```python
import functools
import math

import jax
import jax.numpy as jnp
from jax import lax
from jax.experimental import pallas as pl
from jax.experimental.pallas import tpu as pltpu

F32 = jnp.float32
BF16 = jnp.bfloat16
HIGHEST = lax.Precision.HIGHEST

SB_HEADS = 8
SB_HEAD_DIM = 128
RW_HEAD_DIM = 64
RW_DECAY_LORA = 64
RW_AAA_LORA = 64
RW_GATE_LORA = 160
RW_GN_EPS = 64e-5
GLA_HEADS = 4
GLA_GATE_RANK = 16
GLA_TAU = 16.0
CONV_W = 3
LN_EPS = 1e-5

LANES = 128
SUBLANES = 8
VMEM_LIMIT_BYTES = 56 * 1024 * 1024

MM_TM = 1024
MM_TN = 512
LN_TM = 512
FFN_TM = 1024
FFN_TN = 512
DOWN_TM = 256
SB_TQ = 256
SB_HEADS_PER_STEP = 2
RW_CHUNK = 64
RW_PAIR = LANES // RW_HEAD_DIM
GLA_CHUNK = 64
GLA_TB = 256


def _round_up(n, m):
    return (n + m - 1) // m * m


def _params(*sem):
    return pltpu.CompilerParams(dimension_semantics=sem, vmem_limit_bytes=VMEM_LIMIT_BYTES)


def _log_sigmoid(x):
    return jnp.minimum(x, 0.0) - jnp.log1p(jnp.exp(-jnp.abs(x)))


def _sigmoid(x):
    return 1.0 / (1.0 + jnp.exp(-x))


def _dot(a, b):
    return jnp.dot(a.astype(BF16), b.astype(BF16), preferred_element_type=F32)


def _dot_nt(a, b):
    return lax.dot_general(a.astype(BF16), b.astype(BF16), (((1,), (1,)), ((), ())),
                           preferred_element_type=F32)


def _dot_tn(a, b):
    return lax.dot_general(a.astype(BF16), b.astype(BF16), (((0,), (0,)), ((), ())),
                           preferred_element_type=F32)


def _dot_exact(a, b):
    return jnp.dot(a, b, preferred_element_type=F32, precision=HIGHEST)


def _dot_hilo(x, w01):
    hi = x.astype(BF16)
    lo = (x - hi.astype(F32)).astype(BF16)
    return (jnp.dot(hi, w01, preferred_element_type=F32)
            + jnp.dot(lo, w01, preferred_element_type=F32))


def _layer_norm(y, g, b, eps):
    mu = jnp.mean(y, axis=-1, keepdims=True)
    d = y - mu
    var = jnp.mean(d * d, axis=-1, keepdims=True)
    return d * lax.rsqrt(var + eps) * g + b


def _mm_kernel(a_ref, w_ref, o_ref):
    o_ref[...] = jnp.dot(a_ref[...], w_ref[...], preferred_element_type=F32).astype(o_ref.dtype)


def _matmul(a, w, out_dtype, name):
    m, k = a.shape
    n = w.shape[1]
    tm, tn = MM_TM, MM_TN
    assert m % tm == 0 and n % tn == 0
    return pl.pallas_call(
        _mm_kernel,
        grid=(n // tn, m // tm),
        in_specs=[pl.BlockSpec((tm, k), lambda j, i: (i, 0)),
                  pl.BlockSpec((k, tn), lambda j, i: (0, j))],
        out_specs=pl.BlockSpec((tm, tn), lambda j, i: (i, j)),
        out_shape=jax.ShapeDtypeStruct((m, n), out_dtype),
        compiler_params=_params("arbitrary", "arbitrary"),
        name=name,
    )(a, w)


def _proj_ln_kernel(*refs, n_parts, alpha):
    a_refs = refs[:n_parts]
    w_ref, res_ref, g_ref, b_ref, o_ref, obf_ref = refs[n_parts:]
    acc = None
    k0 = 0
    for a_ref in a_refs:
        kp = a_ref.shape[1]
        part = jnp.dot(a_ref[...], w_ref[k0:k0 + kp, :], preferred_element_type=F32)
        acc = part if acc is None else acc + part
        k0 += kp
    y = alpha * res_ref[...] + acc
    out = _layer_norm(y, g_ref[...], b_ref[...], LN_EPS)
    o_ref[...] = out
    obf_ref[...] = out.astype(BF16)


def _proj_ln(a_parts, w, res, g, b, alpha, tm, name):
    m, d = res.shape
    k = w.shape[0]
    assert m % tm == 0 and sum(a.shape[1] for a in a_parts) == k
    in_specs = [pl.BlockSpec((tm, a.shape[1]), lambda i: (i, 0)) for a in a_parts]
    in_specs += [pl.BlockSpec((k, d), lambda i: (0, 0), pipeline_mode=pl.Buffered(1)),
                 pl.BlockSpec((tm, d), lambda i: (i, 0)),
                 pl.BlockSpec((1, d), lambda i: (0, 0)),
                 pl.BlockSpec((1, d), lambda i: (0, 0))]
    return pl.pallas_call(
        functools.partial(_proj_ln_kernel, n_parts=len(a_parts), alpha=alpha),
        grid=(m // tm,),
        in_specs=in_specs,
        out_specs=[pl.BlockSpec((tm, d), lambda i: (i, 0)),
                   pl.BlockSpec((tm, d), lambda i: (i, 0))],
        out_shape=[jax.ShapeDtypeStruct((m, d), F32), jax.ShapeDtypeStruct((m, d), BF16)],
        compiler_params=_params("arbitrary"),
        name=name,
    )(*a_parts, w, res, g.reshape(1, d), b.reshape(1, d))


def _sb_kernel(q_ref, k_ref, v_ref, o_ref, *, tq, dh, scale):
    i = pl.program_id(2)
    n_h = q_ref.shape[1] // dh
    row = lax.broadcasted_iota(jnp.int32, (tq, tq), 0)
    col = lax.broadcasted_iota(jnp.int32, (tq, tq), 1)
    after = jnp.where(row > col, 1.0, 0.0).astype(BF16)
    causal = col < row

    def tile(j, state, diagonal):
        k0 = pl.multiple_of(j * tq, tq)
        new = []
        for hh in range(n_h):
            carry, acc = state[hh]
            cols = slice(hh * dh, (hh + 1) * dh)
            q = q_ref[:, cols]
            kt = k_ref[pl.ds(k0, tq), cols]
            vt = v_ref[pl.ds(k0, tq), cols]
            z = lax.dot_general(q, kt, (((1,), (1,)), ((), ())),
                                preferred_element_type=F32) * scale
            ls = jnp.minimum(z, 0.0) - jnp.log(1.0 + jnp.exp(-jnp.abs(z)))
            lk = ls - z
            if diagonal:
                lk = jnp.where(causal, lk, 0.0)
            suffix = _dot_hilo(lk, after) + carry
            w = jnp.exp(ls + suffix)
            if diagonal:
                w = jnp.where(causal, w, 0.0)
            acc = acc + jnp.dot(w.astype(BF16), vt, preferred_element_type=F32)
            carry = suffix[:, 0:1] + lk[:, 0:1]
            new.append((carry, acc))
        return tuple(new)

    init = tuple((jnp.zeros((tq, 1), F32), jnp.zeros((tq, dh), F32)) for _ in range(n_h))
    state = tile(i, init, True)
    state = lax.fori_loop(0, i, lambda n, st: tile(i - 1 - n, st, False), state)
    for hh in range(n_h):
        o_ref[:, hh * dh:(hh + 1) * dh] = state[hh][1].astype(o_ref.dtype)


def _sb_attention(u_sb, batch, seq):
    tq = SB_TQ
    nq = seq // tq
    dh = SB_HEAD_DIM
    g = SB_HEADS_PER_STEP
    ng = SB_HEADS // g
    wd = g * dh
    assert seq % tq == 0 and dh % LANES == 0 and SB_HEADS % g == 0
    return pl.pallas_call(
        functools.partial(_sb_kernel, tq=tq, dh=dh, scale=dh ** -0.5),
        grid=(batch, ng, nq),
        in_specs=[pl.BlockSpec((tq, wd), lambda b, hh, i: (b * nq + i, hh)),
                  pl.BlockSpec((seq, wd), lambda b, hh, i: (b, ng + hh)),
                  pl.BlockSpec((seq, wd), lambda b, hh, i: (b, 2 * ng + hh))],
        out_specs=pl.BlockSpec((tq, wd), lambda b, hh, i: (b * nq + i, hh)),
        out_shape=jax.ShapeDtypeStruct((batch * seq, SB_HEADS * dh), BF16),
        compiler_params=_params("arbitrary", "arbitrary", "arbitrary"),
        name="sb_attention",
    )(u_sb, u_sb, u_sb)


def _rwkv_kernel(u_ref, mu_ref, w0_ref, w2_ref, a0_ref, a2_ref, g2_ref, kk_ref, ka_ref, rk_ref,
                 gng_ref, gnb_ref, o_ref, state_ref, prev_ref, *, chunk, width):
    L, C = chunk, width
    n_pairs = C // LANES
    N = LANES // RW_PAIR
    c = pl.program_id(1)

    @pl.when(c == 0)
    def _():
        state_ref[...] = jnp.zeros_like(state_ref)
        prev_ref[...] = jnp.zeros_like(prev_ref)

    u = u_ref[...]
    row1 = lax.broadcasted_iota(jnp.int32, (L, 1), 0)
    prev = jnp.where(row1 == 0, prev_ref[SUBLANES - 1:SUBLANES, :], pltpu.roll(u, 1, axis=0))
    prev_ref[...] = u[L - SUBLANES:, :]
    us = u + (prev - u) * mu_ref[...]

    r = us[:, 0:C]
    k = us[:, C:2 * C]
    v = us[:, 2 * C:3 * C]
    o1 = 3 * C
    o2 = o1 + _round_up(RW_DECAY_LORA, LANES)
    o3 = o2 + _round_up(RW_AAA_LORA, LANES)
    dw = us[:, o1:o2]
    da = us[:, o2:o3]
    dg = us[:, o3:]

    w_log = _log_sigmoid(w0_ref[...] + _dot(jnp.tanh(dw), w2_ref[...])) - 0.5
    log_decay = -jnp.exp(w_log)
    a = _sigmoid(a0_ref[...] + _dot(da, a2_ref[...]))
    gate = _dot(_sigmoid(dg), g2_ref[...])

    rowl = lax.broadcasted_iota(jnp.int32, (L, L), 0)
    coll = lax.broadcasted_iota(jnp.int32, (L, L), 1)
    lc = _dot_exact(jnp.where(rowl >= coll, 1.0, 0.0), log_decay)
    g_incl = jnp.exp(lc)
    g_excl = jnp.exp(lc - log_decay)
    g_inv = jnp.exp(-lc)
    g_last = g_incl[L - 1:L, :]

    hr = lax.broadcasted_iota(jnp.int32, (LANES, LANES), 0) // N
    hc = lax.broadcasted_iota(jnp.int32, (LANES, LANES), 1) // N
    same_head = jnp.where(hr == hc, 1.0, 0.0).astype(BF16)

    def head_sum(x):
        rows = jnp.concatenate([x[:, p * LANES:(p + 1) * LANES] for p in range(n_pairs)], axis=0)
        s = _dot_hilo(rows, same_head)
        return jnp.concatenate([s[p * L:(p + 1) * L, :] for p in range(n_pairs)], axis=1)

    kkf = k * kk_ref[...]
    kk = kkf * lax.rsqrt(jnp.maximum(head_sum(kkf * kkf), 1e-24))
    kmod = k * (1.0 + (a - 1.0) * ka_ref[...])
    at_all = -(kk * g_excl)
    bt_all = kk * a * g_inv
    kt_all = kmod * g_inv
    rt_all = r * g_incl
    be_all = bt_all * g_last
    ke_all = kt_all * g_last
    bonus = head_sum(r * kmod * rk_ref[...]) * v

    lane = lax.broadcasted_iota(jnp.int32, (1, LANES), 1)
    first = lane < N
    t_idx = lax.broadcasted_iota(jnp.int32, (L, LANES), 0)
    j_idx = lax.broadcasted_iota(jnp.int32, (L, LANES), 1) % N
    strict = t_idx > j_idx
    incl = t_idx >= j_idx
    eye = jnp.where(t_idx == j_idx, 1.0, 0.0)

    def bd(x):
        return jnp.concatenate([jnp.where(first, x, 0.0), jnp.where(first, 0.0, x)], axis=0)

    pairs = range(n_pairs)
    sl = [slice(p * LANES, (p + 1) * LANES) for p in pairs]
    lhs = [jnp.concatenate([at_all[:, sl[p]], rt_all[:, sl[p]]], axis=0) for p in pairs]
    s0 = [state_ref[p] for p in pairs]
    g = [_dot_nt(lhs[p], jnp.concatenate([bd(bt_all[:, sl[p]]), bd(kt_all[:, sl[p]])], axis=0))
         for p in pairs]
    npow = [jnp.where(strict, g[p][:L, :LANES], 0.0) for p in pairs]
    a_ak = [jnp.where(strict, g[p][:L, LANES:], 0.0) for p in pairs]
    a_rb = [jnp.where(incl, g[p][L:, :LANES], 0.0) for p in pairs]
    a_rk = [jnp.where(incl, g[p][L:, LANES:], 0.0) for p in pairs]
    ss = [_dot_nt(lhs[p], bd(s0[p])) for p in pairs]
    av = [_dot(jnp.concatenate([a_ak[p], a_rk[p]], axis=0), bd(v[:, sl[p]])) for p in pairs]
    inv = [eye + npow[p] for p in pairs]
    span = 1
    while 2 * span < L:
        npow = [_dot(npow[p], bd(npow[p])) for p in pairs]
        inv = [inv[p] + _dot(inv[p], bd(npow[p])) for p in pairs]
        span *= 2
    uu = [_dot(inv[p], bd(ss[p][:L] + av[p][:L])) for p in pairs]
    y = [ss[p][L:] + av[p][L:] + _dot(a_rb[p], bd(uu[p])) for p in pairs]
    for p in pairs:
        full = _dot_tn(jnp.concatenate([uu[p], v[:, sl[p]]], axis=0),
                       jnp.concatenate([be_all[:, sl[p]], ke_all[:, sl[p]]], axis=0))
        state_ref[p] = s0[p] * g_last[:, sl[p]] + jnp.where(first, full[:N], full[N:])

    y = jnp.concatenate(y, axis=1)
    d = y - head_sum(y) * (1.0 / N)
    var = head_sum(d * d) * (1.0 / N)
    out = d * lax.rsqrt(var + RW_GN_EPS) * gng_ref[...] + gnb_ref[...]
    o_ref[...] = ((out + bonus) * gate).astype(o_ref.dtype)


def _rwkv(u_rw, batch, seq, mu, w0, w2, a0, a2, g2, k_k, k_a, r_k, gn_g, gn_b):
    L = RW_CHUNK
    width = w0.shape[0]
    n = RW_HEAD_DIM
    nc = seq // L
    wu = u_rw.shape[1]
    assert seq % L == 0 and L % SUBLANES == 0 and width % LANES == 0 and L == n
    row = lambda x: x.reshape(1, -1)
    const = lambda shape: pl.BlockSpec(shape, lambda b, c: (0, 0))
    return pl.pallas_call(
        functools.partial(_rwkv_kernel, chunk=L, width=width),
        grid=(batch, nc),
        in_specs=[pl.BlockSpec((L, wu), lambda b, c: (b * nc + c, 0)),
                  const((1, wu)), const((1, width)), const(w2.shape), const((1, width)),
                  const(a2.shape), const(g2.shape), const((1, width)), const((1, width)),
                  const((1, width)), const((1, width)), const((1, width))],
        out_specs=pl.BlockSpec((L, width), lambda b, c: (b * nc + c, 0)),
        out_shape=jax.ShapeDtypeStruct((batch * seq, width), BF16),
        scratch_shapes=[pltpu.VMEM((width // LANES, n, LANES), F32),
                        pltpu.VMEM((SUBLANES, wu), F32)],
        compiler_params=_params("arbitrary", "arbitrary"),
        name="rwkv7",
    )(u_rw, row(mu), row(w0), w2, row(a0), a2, g2, row(k_k), row(k_a), row(r_k), row(gn_g),
      row(gn_b))


def _gla_kernel(q_ref, k_ref, v_ref, r_ref, dg_ref, gw_ref, gb_ref, rb_ref, gng_ref, gnb_ref,
                o_ref, state_ref, *, chunk, scale):
    L = chunk
    tb = q_ref.shape[0]
    t = pl.program_id(2)

    @pl.when(t == 0)
    def _():
        state_ref[...] = jnp.zeros_like(state_ref)

    log_a = _log_sigmoid(_dot_exact(dg_ref[...], gw_ref[...]) + gb_ref[...]) / GLA_TAU
    row = lax.broadcasted_iota(jnp.int32, (L, L), 0)
    col = lax.broadcasted_iota(jnp.int32, (L, L), 1)
    incl = row >= col
    lower = jnp.where(incl, 1.0, 0.0)
    for cc in range(tb // L):
        rows = slice(cc * L, (cc + 1) * L)
        b = _dot_exact(lower, log_a[rows, :])
        b_last = b[L - 1:L, :]
        q = q_ref[rows, :] * scale
        k = k_ref[rows, :]
        v = v_ref[rows, :]
        q_dec = q * jnp.exp(b)
        k_inv = k * jnp.exp(-b)
        k_end = k * jnp.exp(b_last - b)
        att = jnp.where(incl, _dot_nt(q_dec, k_inv), 0.0)
        st = state_ref[...]
        o = _dot(att, v) + _dot_nt(q_dec, st)
        state_ref[...] = st * jnp.exp(b_last) + _dot_tn(v, k_end)
        o = _layer_norm(o, gng_ref[...], gnb_ref[...], LN_EPS)
        x = r_ref[rows, :] + rb_ref[...]
        o_ref[rows, :] = (o * (x * _sigmoid(x))).astype(o_ref.dtype)


def _gla(u, dg, batch, seq, gate_w2, gate_b, r_b, gn_g, gn_b):
    tb = GLA_TB
    h = GLA_HEADS
    key = gate_b.shape[0]
    val = r_b.shape[0]
    dk, dv = key // h, val // h
    nt = seq // tb
    rank = dg.shape[1]
    assert seq % tb == 0 and tb % GLA_CHUNK == 0 and key % dv == 0
    kv = key // dv
    row = lambda x: x.reshape(1, -1)
    return pl.pallas_call(
        functools.partial(_gla_kernel, chunk=GLA_CHUNK, scale=dk ** -0.5),
        grid=(batch, h, nt),
        in_specs=[pl.BlockSpec((tb, dk), lambda b, hh, t: (b * nt + t, hh)),
                  pl.BlockSpec((tb, dk), lambda b, hh, t: (b * nt + t, h + hh)),
                  pl.BlockSpec((tb, dv), lambda b, hh, t: (b * nt + t, 2 * kv + hh)),
                  pl.BlockSpec((tb, dv), lambda b, hh, t: (b * nt + t, 2 * kv + h + hh)),
                  pl.BlockSpec((tb, rank), lambda b, hh, t: (b * nt + t, 0)),
                  pl.BlockSpec((rank, dk), lambda b, hh, t: (0, hh)),
                  pl.BlockSpec((1, dk), lambda b, hh, t: (0, hh)),
                  pl.BlockSpec((1, dv), lambda b, hh, t: (0, hh)),
                  pl.BlockSpec((1, dv), lambda b, hh, t: (0, hh)),
                  pl.BlockSpec((1, dv), lambda b, hh, t: (0, hh))],
        out_specs=pl.BlockSpec((tb, dv), lambda b, hh, t: (b * nt + t, hh)),
        out_shape=jax.ShapeDtypeStruct((batch * seq, val), BF16),
        scratch_shapes=[pltpu.VMEM((dv, dk), F32)],
        compiler_params=_params("arbitrary", "arbitrary", "arbitrary"),
        name="gla",
    )(u, u, u, u, dg, gate_w2, row(gate_b), row(r_b), row(gn_g), row(gn_b))


def _ffn_up_kernel(a_ref, wg_ref, wu_ref, cw_ref, cb_ref, o_ref, carry_ref, *, seq):
    i = pl.program_id(1)
    tm = a_ref.shape[0]
    a = a_ref[...]
    gate = jnp.dot(a, wg_ref[...], preferred_element_type=F32)
    up = jnp.dot(a, wu_ref[...], preferred_element_type=F32)
    first = (i * tm) % seq == 0
    tail = carry_ref[...]
    carry_ref[...] = gate[tm - SUBLANES:, :]
    p1 = jnp.where(first, 0.0, tail[SUBLANES - 1:SUBLANES, :])
    p2 = jnp.where(first, 0.0, tail[SUBLANES - 2:SUBLANES - 1, :])
    row = lax.broadcasted_iota(jnp.int32, (tm, 1), 0)
    g1 = jnp.where(row == 0, p1, pltpu.roll(gate, 1, axis=0))
    g2 = jnp.where(row == 0, p2, jnp.where(row == 1, p1, pltpu.roll(gate, 2, axis=0)))
    conv = cw_ref[0:1, :] * g2 + cw_ref[1:2, :] * g1 + cw_ref[2:3, :] * gate + cb_ref[...]
    act = 0.5 * conv * (1.0 + lax.erf(conv * math.sqrt(0.5)))
    o_ref[...] = (act * up).astype(o_ref.dtype)


def _ffn_up(a, w_up, conv_w, conv_b, seq):
    m, k = a.shape
    dff = conv_b.shape[0]
    tm, tn = FFN_TM, FFN_TN
    nj = dff // tn
    assert m % tm == 0 and dff % tn == 0 and seq % tm == 0 and conv_w.shape[0] == CONV_W
    return pl.pallas_call(
        functools.partial(_ffn_up_kernel, seq=seq),
        grid=(nj, m // tm),
        in_specs=[pl.BlockSpec((tm, k), lambda j, i: (i, 0)),
                  pl.BlockSpec((k, tn), lambda j, i: (0, j)),
                  pl.BlockSpec((k, tn), lambda j, i: (0, nj + j)),
                  pl.BlockSpec((CONV_W, tn), lambda j, i: (0, j)),
                  pl.BlockSpec((1, tn), lambda j, i: (0, j))],
        out_specs=pl.BlockSpec((tm, tn), lambda j, i: (i, j)),
        out_shape=jax.ShapeDtypeStruct((m, dff), BF16),
        scratch_shapes=[pltpu.VMEM((SUBLANES, tn), F32)],
        compiler_params=_params("arbitrary", "arbitrary"),
        name="ffn_up",
    )(a, w_up, w_up, conv_w, conv_b.reshape(1, dff))


def _pad_cols(w, n):
    return jnp.pad(w, ((0, 0), (0, n - w.shape[1])))


def _pad_rows(w, n):
    return jnp.pad(w, ((0, n - w.shape[0]), (0, 0)))


def _pad_vec(w, n):
    return jnp.pad(w, (0, n - w.shape[0]))


def kernel(x, even_w_in, even_shift_mu, rw_w0, rw_w2, rw_a0, rw_a2, rw_g2, rw_k_k, rw_k_a, rw_r_k, rw_gn_g, rw_gn_b, even_w_out, odd_w_in, gla_gate_w2, gla_gate_b, gla_r_b, gla_gn_g, gla_gn_b, odd_w_out, ln_mix_g, ln_mix_b, ln_ffn_g, ln_ffn_b, ffn_w_up, ffn_conv_w, ffn_conv_b, ffn_w_down):
    batch, seq, d = x.shape
    depth = ln_mix_g.shape[0]
    alpha = (2 * depth) ** 0.25
    m = batch * seq
    h = x.reshape(m, d)
    h_bf = h.astype(BF16)

    sb_w = SB_HEADS * SB_HEAD_DIM
    rw_w = rw_w0.shape[1]
    lw, la, lg = (_round_up(n, LANES) for n in (RW_DECAY_LORA, RW_AAA_LORA, RW_GATE_LORA))

    for layer in range(depth):
        i = layer // 2
        if layer % 2 == 0:
            w_in = even_w_in[i]
            c0 = 3 * sb_w
            c1 = c0 + 3 * rw_w
            c2 = c1 + RW_DECAY_LORA
            c3 = c2 + RW_AAA_LORA
            w_rw = jnp.concatenate([w_in[:, c0:c1], _pad_cols(w_in[:, c1:c2], lw),
                                    _pad_cols(w_in[:, c2:c3], la), _pad_cols(w_in[:, c3:], lg)],
                                   axis=1)
            mu = even_shift_mu[i]
            m0 = 3 * rw_w
            m1 = m0 + RW_DECAY_LORA
            m2 = m1 + RW_AAA_LORA
            mu_p = jnp.concatenate([mu[:m0], _pad_vec(mu[m0:m1], lw), _pad_vec(mu[m1:m2], la),
                                    _pad_vec(mu[m2:], lg)])
            u_sb = _matmul(h_bf, w_in[:, :c0].astype(BF16), BF16, "even_in_sb")
            u_rw = _matmul(h_bf, w_rw.astype(BF16), F32, "even_in_rw")
            o_sb = _sb_attention(u_sb, batch, seq)
            o_rw = _rwkv(u_rw, batch, seq, mu_p, rw_w0[i],
                         _pad_rows(rw_w2[i], lw).astype(BF16), rw_a0[i],
                         _pad_rows(rw_a2[i], la).astype(BF16),
                         _pad_rows(rw_g2[i], lg).astype(BF16),
                         rw_k_k[i], rw_k_a[i], rw_r_k[i].reshape(-1), rw_gn_g[i], rw_gn_b[i])
            h, h_bf = _proj_ln([o_sb, o_rw], even_w_out[i].astype(BF16), h, ln_mix_g[layer],
                               ln_mix_b[layer], alpha, LN_TM, "even_out_ln")
        else:
            w_in = odd_w_in[i]
            key = gla_gate_b.shape[1]
            val = gla_r_b.shape[1]
            c0 = 2 * key + val
            c1 = c0 + GLA_GATE_RANK
            rank_p = _round_up(GLA_GATE_RANK, LANES)
            w_main = jnp.concatenate([w_in[:, :c0], w_in[:, c1:]], axis=1)
            w_dg = _pad_cols(w_in[:, c0:c1], MM_TN)
            u = _matmul(h_bf, w_main.astype(BF16), F32, "odd_in")
            dg = _matmul(h_bf, w_dg.astype(BF16), F32, "odd_in_gate")[:, :rank_p]
            o = _gla(u, dg, batch, seq, _pad_rows(gla_gate_w2[i], rank_p), gla_gate_b[i],
                     gla_r_b[i], gla_gn_g[i], gla_gn_b[i])
            h, h_bf = _proj_ln([o], odd_w_out[i].astype(BF16), h, ln_mix_g[layer],
                               ln_mix_b[layer], alpha, LN_TM, "odd_out_ln")
        mid = _ffn_up(h_bf, ffn_w_up[layer].astype(BF16), ffn_conv_w[layer], ffn_conv_b[layer],
                      seq)
        h, h_bf = _proj_ln([mid], ffn_w_down[layer].astype(BF16), h, ln_ffn_g[layer],
                           ln_ffn_b[layer], alpha, DOWN_TM, "ffn_down_ln")
    return h.reshape(batch, seq, d)
```

```python
import functools
import math

import jax
import jax.numpy as jnp
from jax import lax
from jax.experimental import pallas as pl
from jax.experimental.pallas import tpu as pltpu

F32 = jnp.float32
BF16 = jnp.bfloat16

SB_HEADS = 8
SB_HEAD_DIM = 128
RW_HEAD_DIM = 64
RW_DECAY_LORA = 64
RW_AAA_LORA = 64
RW_GATE_LORA = 160
RW_GN_EPS = 64e-5
GLA_HEADS = 4
GLA_GATE_RANK = 16
GLA_TAU = 16.0
CONV_W = 3
LN_EPS = 1e-5

LANES = 128
SUBLANES = 8
VMEM_LIMIT_BYTES = 56 * 1024 * 1024

MM_TM = 1024
MM_TN = 512
LN_TM = 512
FFN_TM = 1024
FFN_TN = 512
DOWN_TM = 256
CAST_ROWS = 256
SB_TQ = 256
SB_HEADS_PER_STEP = 4
RW_CHUNK = 64
RW_PAIR = LANES // RW_HEAD_DIM
RW_BATCH = 2
GLA_CHUNK = 64
GLA_TB = 256
GLA_HEADS_PER_STEP = 2


def _round_up(n, m):
    return (n + m - 1) // m * m


def _params(*sem):
    return pltpu.CompilerParams(dimension_semantics=sem, vmem_limit_bytes=VMEM_LIMIT_BYTES)


def _log_sigmoid(x):
    return jnp.minimum(x, 0.0) - jnp.log(1.0 + jnp.exp(-jnp.abs(x)))


def _sigmoid(x):
    return 1.0 / (1.0 + jnp.exp(-x))


def _dot(a, b):
    return jnp.dot(a.astype(BF16), b.astype(BF16), preferred_element_type=F32)


def _dot_nt(a, b):
    return lax.dot_general(a.astype(BF16), b.astype(BF16), (((1,), (1,)), ((), ())),
                           preferred_element_type=F32)


def _dot_tn(a, b):
    return lax.dot_general(a.astype(BF16), b.astype(BF16), (((0,), (0,)), ((), ())),
                           preferred_element_type=F32)


def _split2(x):
    hi = x.astype(BF16)
    return hi, (x - hi.astype(F32)).astype(BF16)


def _split3(x):
    hi = x.astype(BF16)
    r = x - hi.astype(F32)
    mid = r.astype(BF16)
    return hi, mid, (r - mid.astype(F32)).astype(BF16)


def _dot_hilo(x, w01):
    hi, lo = _split2(x)
    return (jnp.dot(hi, w01, preferred_element_type=F32)
            + jnp.dot(lo, w01, preferred_element_type=F32))


def _cumsum_dot(w01, x):
    return sum(jnp.dot(w01, p, preferred_element_type=F32) for p in _split3(x))


def _dot_split(x, w):
    xh, xl = _split2(x)
    wh, wl = _split2(w)
    return (jnp.dot(xh, wh, preferred_element_type=F32) + jnp.dot(xh, wl, preferred_element_type=F32)
            + jnp.dot(xl, wh, preferred_element_type=F32))


def _layer_norm(y, g, b, eps):
    mu = jnp.mean(y, axis=-1, keepdims=True)
    d = y - mu
    var = jnp.mean(d * d, axis=-1, keepdims=True)
    return d * lax.rsqrt(var + eps) * g + b


def _mm_kernel(a_ref, w_ref, o_ref, wbf_ref):
    @pl.when(pl.program_id(1) == 0)
    def _():
        wbf_ref[...] = w_ref[...].astype(BF16)

    o_ref[...] = jnp.dot(a_ref[...], wbf_ref[...],
                         preferred_element_type=F32).astype(o_ref.dtype)


def _matmul(a, w3, layer, tile0, n_out, out_dtype, name):
    m, k = a.shape
    tm, tn = MM_TM, MM_TN
    assert m % tm == 0 and n_out % tn == 0 and w3.shape[1] == k
    return pl.pallas_call(
        _mm_kernel,
        grid=(n_out // tn, m // tm),
        in_specs=[pl.BlockSpec((tm, k), lambda j, i: (i, 0)),
                  pl.BlockSpec((None, k, tn), lambda j, i: (layer, 0, tile0 + j))],
        out_specs=pl.BlockSpec((tm, tn), lambda j, i: (i, j)),
        out_shape=jax.ShapeDtypeStruct((m, n_out), out_dtype),
        scratch_shapes=[pltpu.VMEM((k, tn), BF16)],
        compiler_params=_params("arbitrary", "arbitrary"),
        name=name,
    )(a, w3)


def _proj_ln_kernel(*refs, n_parts, n_cast, alpha):
    a_refs = refs[:n_parts]
    w_ref, res_ref, g_ref, b_ref, o_ref, obf_ref, wbf_ref = refs[n_parts:]
    s = pl.program_id(0)
    rows = w_ref.shape[0]

    @pl.when(s < n_cast)
    def _():
        r0 = pl.multiple_of(s * rows, rows)
        wbf_ref[pl.ds(r0, rows), :] = w_ref[...].astype(BF16)

    @pl.when(s >= n_cast)
    def _():
        acc = None
        k0 = 0
        for a_ref in a_refs:
            kp = a_ref.shape[1]
            part = jnp.dot(a_ref[...], wbf_ref[k0:k0 + kp, :], preferred_element_type=F32)
            acc = part if acc is None else acc + part
            k0 += kp
        y = alpha * res_ref[...] + acc
        out = _layer_norm(y, g_ref[...], b_ref[...], LN_EPS)
        o_ref[...] = out
        obf_ref[...] = out.astype(BF16)


def _proj_ln(a_parts, w3, layer, res, g, b, alpha, tm, name):
    m, d = res.shape
    k = w3.shape[1]
    rows = CAST_ROWS
    n_cast = k // rows
    assert m % tm == 0 and k % rows == 0 and sum(a.shape[1] for a in a_parts) == k
    tok = lambda s: (jnp.maximum(s - n_cast, 0), 0)
    in_specs = [pl.BlockSpec((tm, a.shape[1]), tok) for a in a_parts]
    in_specs += [pl.BlockSpec((None, rows, d), lambda s: (layer, jnp.minimum(s, n_cast - 1), 0)),
                 pl.BlockSpec((tm, d), tok),
                 pl.BlockSpec((1, d), lambda s: (0, 0)),
                 pl.BlockSpec((1, d), lambda s: (0, 0))]
    return pl.pallas_call(
        functools.partial(_proj_ln_kernel, n_parts=len(a_parts), n_cast=n_cast, alpha=alpha),
        grid=(n_cast + m // tm,),
        in_specs=in_specs,
        out_specs=[pl.BlockSpec((tm, d), tok), pl.BlockSpec((tm, d), tok)],
        out_shape=[jax.ShapeDtypeStruct((m, d), F32), jax.ShapeDtypeStruct((m, d), BF16)],
        scratch_shapes=[pltpu.VMEM((k, d), BF16)],
        compiler_params=_params("arbitrary"),
        name=name,
    )(*a_parts, w3, res, g.reshape(1, d), b.reshape(1, d))


def _sb_kernel(q_ref, k_ref, v_ref, o_ref, *, tq, dh, scale):
    i = pl.program_id(2)
    n_h = q_ref.shape[1] // dh
    heads = range(n_h)
    cols = [slice(hh * dh, (hh + 1) * dh) for hh in heads]
    row = lax.broadcasted_iota(jnp.int32, (tq, tq), 0)
    col = lax.broadcasted_iota(jnp.int32, (tq, tq), 1)
    after = jnp.where(row > col, 1.0, 0.0).astype(BF16)
    causal = col < row

    def tile(j, state, diagonal):
        k0 = pl.multiple_of(j * tq, tq)
        z = [lax.dot_general(q_ref[:, cols[hh]], k_ref[pl.ds(k0, tq), cols[hh]],
                             (((1,), (1,)), ((), ())), preferred_element_type=F32) * scale
             for hh in heads]
        ls = [_log_sigmoid(z[hh]) for hh in heads]
        lk = [ls[hh] - z[hh] for hh in heads]
        if diagonal:
            lk = [jnp.where(causal, lk[hh], 0.0) for hh in heads]
        suffix = [_dot_hilo(lk[hh], after) + state[hh][0] for hh in heads]
        w = [jnp.exp(ls[hh] + suffix[hh]) for hh in heads]
        if diagonal:
            w = [jnp.where(causal, w[hh], 0.0) for hh in heads]
        acc = [state[hh][1] + jnp.dot(w[hh].astype(BF16), v_ref[pl.ds(k0, tq), cols[hh]],
                                      preferred_element_type=F32) for hh in heads]
        carry = [suffix[hh][:, 0:1] + lk[hh][:, 0:1] for hh in heads]
        return tuple((carry[hh], acc[hh]) for hh in heads)

    init = tuple((jnp.zeros((tq, 1), F32), jnp.zeros((tq, dh), F32)) for _ in heads)
    state = tile(i, init, True)
    state = lax.fori_loop(0, i, lambda n, st: tile(i - 1 - n, st, False), state)
    for hh in heads:
        o_ref[:, cols[hh]] = state[hh][1].astype(o_ref.dtype)


def _sb_attention(u_sb, batch, seq):
    tq = SB_TQ
    nq = seq // tq
    dh = SB_HEAD_DIM
    g = SB_HEADS_PER_STEP
    ng = SB_HEADS // g
    wd = g * dh
    assert seq % tq == 0 and dh % LANES == 0 and SB_HEADS % g == 0
    return pl.pallas_call(
        functools.partial(_sb_kernel, tq=tq, dh=dh, scale=dh ** -0.5),
        grid=(batch, ng, nq),
        in_specs=[pl.BlockSpec((tq, wd), lambda b, hh, i: (b * nq + i, hh)),
                  pl.BlockSpec((seq, wd), lambda b, hh, i: (b, ng + hh)),
                  pl.BlockSpec((seq, wd), lambda b, hh, i: (b, 2 * ng + hh))],
        out_specs=pl.BlockSpec((tq, wd), lambda b, hh, i: (b * nq + i, hh)),
        out_shape=jax.ShapeDtypeStruct((batch * seq, SB_HEADS * dh), BF16),
        compiler_params=_params("arbitrary", "arbitrary", "arbitrary"),
        name="sb_attention",
    )(u_sb, u_sb, u_sb)


def _token_shift(u, prev_ref, mu):
    rows = u.shape[0]
    row1 = lax.broadcasted_iota(jnp.int32, (rows, 1), 0)
    prev = jnp.where(row1 == 0, prev_ref[SUBLANES - 1:SUBLANES, :], pltpu.roll(u, 1, axis=0))
    prev_ref[...] = u[rows - SUBLANES:, :]
    return u + (prev - u) * mu


def _rwkv_kernel(u_ref, ul_ref, mu_ref, mul_ref, w0_ref, w2_ref, a0_ref, a2_ref, g2_ref, kk_ref,
                 ka_ref, rk_ref, gng_ref, gnb_ref, o_ref, state_ref, prev_ref, prevl_ref,
                 *, chunk, width):
    L, C = chunk, width
    nb = u_ref.shape[0]
    n_pairs = C // LANES
    N = LANES // RW_PAIR
    R = nb * L
    c = pl.program_id(1)

    @pl.when(c == 0)
    def _():
        state_ref[...] = jnp.zeros_like(state_ref)
        prev_ref[...] = jnp.zeros_like(prev_ref)
        prevl_ref[...] = jnp.zeros_like(prevl_ref)

    us = jnp.concatenate([_token_shift(u_ref[bb], prev_ref.at[bb], mu_ref[...])
                          for bb in range(nb)], axis=0)
    ul = jnp.concatenate([_token_shift(ul_ref[bb], prevl_ref.at[bb], mul_ref[...])
                          for bb in range(nb)], axis=0)
    r = us[:, 0:C]
    k = us[:, C:2 * C]
    v = us[:, 2 * C:3 * C]
    o1 = _round_up(RW_DECAY_LORA, LANES)
    o2 = o1 + _round_up(RW_AAA_LORA, LANES)
    dw = ul[:, :o1]
    da = ul[:, o1:o2]
    dg = ul[:, o2:]

    w_log = _log_sigmoid(w0_ref[...] + _dot(jnp.tanh(dw), w2_ref[...])) - 0.5
    log_decay = -jnp.exp(w_log)
    a = _sigmoid(a0_ref[...] + _dot(da, a2_ref[...]))
    gate = _dot(_sigmoid(dg), g2_ref[...])

    rowl = lax.broadcasted_iota(jnp.int32, (R, R), 0)
    coll = lax.broadcasted_iota(jnp.int32, (R, R), 1)
    lower01 = jnp.where(coll <= rowl, jnp.where(coll >= (rowl // L) * L, 1.0, 0.0),
                        0.0).astype(BF16)
    lc = _cumsum_dot(lower01, log_decay)
    g_incl = jnp.exp(lc)
    g_excl = jnp.exp(lc - log_decay)
    g_inv = jnp.exp(-lc)
    g_last = [g_incl[(bb + 1) * L - 1:(bb + 1) * L, :] for bb in range(nb)]
    g_last_rows = jnp.concatenate([jnp.broadcast_to(g_last[bb], (L, C)) for bb in range(nb)],
                                  axis=0)

    lane = lax.broadcasted_iota(jnp.int32, (1, LANES), 1)
    first = lane < N

    def head_sum(x):
        out = []
        for p in range(n_pairs):
            xs = x[:, p * LANES:(p + 1) * LANES]
            s1 = jnp.sum(jnp.where(first, xs, 0.0), axis=-1, keepdims=True)
            s2 = jnp.sum(jnp.where(first, 0.0, xs), axis=-1, keepdims=True)
            out.append(jnp.where(first, s1, s2))
        return jnp.concatenate(out, axis=1)

    kkf = k * kk_ref[...]
    kk = kkf * lax.rsqrt(jnp.maximum(head_sum(kkf * kkf), 1e-24))
    kmod = k * (1.0 + (a - 1.0) * ka_ref[...])
    at_all = -(kk * g_excl)
    bt_all = kk * a * g_inv
    kt_all = kmod * g_inv
    rt_all = r * g_incl
    be_all = bt_all * g_last_rows
    ke_all = kt_all * g_last_rows
    bonus = head_sum(r * kmod * rk_ref[...]) * v

    t_idx = lax.broadcasted_iota(jnp.int32, (L, LANES), 0)
    j_idx = lax.broadcasted_iota(jnp.int32, (L, LANES), 1)
    strict = t_idx > j_idx % N
    incl = t_idx >= j_idx % N
    keep1 = jnp.where(j_idx < N, 1.0, 0.0).astype(BF16)
    keep2 = jnp.where(j_idx < N, 0.0, 1.0).astype(BF16)

    def bd(x):
        xb = x.astype(BF16)
        return jnp.concatenate([xb * keep1, xb * keep2], axis=0)

    units = [(bb, p) for bb in range(nb) for p in range(n_pairs)]
    nu = range(len(units))
    rs = [slice(bb * L, (bb + 1) * L) for bb, _ in units]
    sl = [slice(p * LANES, (p + 1) * LANES) for _, p in units]
    lhs = [jnp.concatenate([at_all[rs[u], sl[u]], rt_all[rs[u], sl[u]]], axis=0) for u in nu]
    s0 = [state_ref[bb, p] for bb, p in units]
    g = [_dot_nt(lhs[u], jnp.concatenate([bd(bt_all[rs[u], sl[u]]), bd(kt_all[rs[u], sl[u]])],
                                         axis=0)) for u in nu]
    ss = [_dot_nt(lhs[u], bd(s0[u])) for u in nu]
    pw = [jnp.where(strict, g[u][:L, :LANES], 0.0) for u in nu]
    a_ak = [jnp.where(strict, g[u][:L, LANES:], 0.0) for u in nu]
    a_rb = [jnp.where(incl, g[u][L:, :LANES], 0.0) for u in nu]
    a_rk = [jnp.where(incl, g[u][L:, LANES:], 0.0) for u in nu]
    av = [_dot(jnp.concatenate([a_ak[u], a_rk[u]], axis=0), bd(v[rs[u], sl[u]])) for u in nu]
    nxt = [_dot(pw[u], bd(pw[u])) for u in nu]
    x = [ss[u][:L] + av[u][:L] for u in nu]
    span = 1
    while True:
        x = [x[u] + _dot(pw[u], bd(x[u])) for u in nu]
        if 2 * span >= L:
            break
        pw = nxt
        span *= 2
        if 2 * span < L:
            nxt = [_dot(pw[u], bd(pw[u])) for u in nu]
    y = [ss[u][L:] + av[u][L:] + _dot(a_rb[u], bd(x[u])) for u in nu]
    for u, (bb, p) in enumerate(units):
        full = _dot_tn(jnp.concatenate([x[u], v[rs[u], sl[u]]], axis=0),
                       jnp.concatenate([be_all[rs[u], sl[u]], ke_all[rs[u], sl[u]]], axis=0))
        state_ref[bb, p] = s0[u] * g_last[bb][:, sl[u]] + jnp.where(first, full[:N], full[N:])

    y = jnp.concatenate([jnp.concatenate(y[bb * n_pairs:(bb + 1) * n_pairs], axis=1)
                         for bb in range(nb)], axis=0)
    d = y - head_sum(y) * (1.0 / N)
    var = head_sum(d * d) * (1.0 / N)
    out = (d * lax.rsqrt(var + RW_GN_EPS) * gng_ref[...] + gnb_ref[...] + bonus) * gate
    for bb in range(nb):
        o_ref[bb] = out[bb * L:(bb + 1) * L, :].astype(o_ref.dtype)


def _rwkv(u_rkv, u_lora, batch, seq, mu, mu_lora, w0, w2, a0, a2, g2, k_k, k_a, r_k, gn_g, gn_b):
    L = RW_CHUNK
    nb = RW_BATCH
    width = w0.shape[0]
    n = RW_HEAD_DIM
    nc = seq // L
    wu = u_rkv.shape[1]
    wl = u_lora.shape[1]
    assert seq % L == 0 and L % SUBLANES == 0 and width % LANES == 0 and L == n
    assert wu == 3 * width and batch % nb == 0
    row = lambda x: x.reshape(1, -1)
    const = lambda shape: pl.BlockSpec(shape, lambda b, c: (0, 0))
    out = pl.pallas_call(
        functools.partial(_rwkv_kernel, chunk=L, width=width),
        grid=(batch // nb, nc),
        in_specs=[pl.BlockSpec((nb, L, wu), lambda b, c: (b, c, 0)),
                  pl.BlockSpec((nb, L, wl), lambda b, c: (b, c, 0)),
                  const((1, wu)), const((1, wl)), const((1, width)), const(w2.shape),
                  const((1, width)), const(a2.shape), const(g2.shape), const((1, width)),
                  const((1, width)), const((1, width)), const((1, width)), const((1, width))],
        out_specs=pl.BlockSpec((nb, L, width), lambda b, c: (b, c, 0)),
        out_shape=jax.ShapeDtypeStruct((batch, seq, width), BF16),
        scratch_shapes=[pltpu.VMEM((nb, width // LANES, n, LANES), F32),
                        pltpu.VMEM((nb, SUBLANES, wu), F32), pltpu.VMEM((nb, SUBLANES, wl), F32)],
        compiler_params=_params("arbitrary", "arbitrary"),
        name="rwkv7",
    )(u_rkv.reshape(batch, seq, wu), u_lora.reshape(batch, seq, wl), row(mu), row(mu_lora),
      row(w0), w2, row(a0), a2, g2, row(k_k), row(k_a), row(r_k), row(gn_g), row(gn_b))
    return out.reshape(batch * seq, width)


def _gla_kernel(q_ref, k_ref, v_ref, r_ref, dg_ref, gw_ref, gb_ref, rb_ref, gng_ref, gnb_ref,
                o_ref, state_ref, *, chunk, n_h, scale):
    L = chunk
    tb = q_ref.shape[0]
    dk = q_ref.shape[1] // n_h
    dv = v_ref.shape[1] // n_h
    n_ch = tb // L
    heads = range(n_h)
    chunks = range(n_ch)
    kc = [slice(hh * dk, (hh + 1) * dk) for hh in heads]
    vc = [slice(hh * dv, (hh + 1) * dv) for hh in heads]
    rc = [slice(c * L, (c + 1) * L) for c in chunks]
    t = pl.program_id(2)

    @pl.when(t == 0)
    def _():
        state_ref[...] = jnp.zeros_like(state_ref)

    row = lax.broadcasted_iota(jnp.int32, (tb, tb), 0)
    col = lax.broadcasted_iota(jnp.int32, (tb, tb), 1)
    chunk_start = (row // L) * L
    lower = jnp.where(col <= row, jnp.where(col >= chunk_start, 1.0, 0.0), 0.0)
    lower01 = lower.astype(BF16)
    in_chunk = lower > 0.0

    log_a = _log_sigmoid(_dot_split(dg_ref[...], gw_ref[...]) + gb_ref[...]) / GLA_TAU
    b = _cumsum_dot(lower01, log_a)
    eb = jnp.exp(b)
    enb = jnp.exp(-b)
    decay = [eb[(c + 1) * L - 1:(c + 1) * L, :] for c in chunks]
    eb_last = jnp.concatenate([jnp.broadcast_to(decay[c], (L, n_h * dk)) for c in chunks], axis=0)
    q_dec = (q_ref[...] * scale * eb).astype(BF16)
    k_inv = k_ref[...] * enb
    k_end = (k_inv * eb_last).astype(BF16)
    k_inv = k_inv.astype(BF16)
    v = v_ref[...].astype(BF16)
    att = [jnp.where(in_chunk, _dot_nt(q_dec[:, kc[hh]], k_inv[:, kc[hh]]), 0.0) for hh in heads]
    o = [_dot(att[hh], v[:, vc[hh]]) for hh in heads]
    kv = [[_dot_tn(v[rc[c], vc[hh]], k_end[rc[c], kc[hh]]) for c in chunks] for hh in heads]
    st = [state_ref[hh] for hh in heads]
    inter = [[] for _ in heads]
    for c in chunks:
        for hh in heads:
            inter[hh].append(_dot_nt(q_dec[rc[c], kc[hh]], st[hh]))
            st[hh] = st[hh] * decay[c][:, kc[hh]] + kv[hh][c]
    for hh in heads:
        state_ref[hh] = st[hh]
        oh = o[hh] + jnp.concatenate(inter[hh], axis=0)
        oh = _layer_norm(oh, gng_ref[:, vc[hh]], gnb_ref[:, vc[hh]], LN_EPS)
        x = r_ref[:, vc[hh]] + rb_ref[:, vc[hh]]
        o_ref[:, vc[hh]] = (oh * (x * _sigmoid(x))).astype(o_ref.dtype)


def _gla(u_qkv, u_rg, batch, seq, gate_w2, gate_b, r_b, gn_g, gn_b):
    tb = GLA_TB
    g = GLA_HEADS_PER_STEP
    key = gate_b.shape[0]
    val = r_b.shape[0]
    dk, dv = key // GLA_HEADS, val // GLA_HEADS
    gk, gv = g * dk, g * dv
    ng = GLA_HEADS // g
    nt = seq // tb
    rank = gate_w2.shape[0]
    assert seq % tb == 0 and tb % GLA_CHUNK == 0 and GLA_HEADS % g == 0
    assert (2 * key) % gv == 0 and val % rank == 0
    row = lambda x: x.reshape(1, -1)
    return pl.pallas_call(
        functools.partial(_gla_kernel, chunk=GLA_CHUNK, n_h=g, scale=dk ** -0.5),
        grid=(batch, ng, nt),
        in_specs=[pl.BlockSpec((tb, gk), lambda b, hh, t: (b * nt + t, hh)),
                  pl.BlockSpec((tb, gk), lambda b, hh, t: (b * nt + t, ng + hh)),
                  pl.BlockSpec((tb, gv), lambda b, hh, t: (b * nt + t, 2 * key // gv + hh)),
                  pl.BlockSpec((tb, gv), lambda b, hh, t: (b * nt + t, hh)),
                  pl.BlockSpec((tb, rank), lambda b, hh, t: (b * nt + t, val // rank)),
                  pl.BlockSpec((rank, gk), lambda b, hh, t: (0, hh)),
                  pl.BlockSpec((1, gk), lambda b, hh, t: (0, hh)),
                  pl.BlockSpec((1, gv), lambda b, hh, t: (0, hh)),
                  pl.BlockSpec((1, gv), lambda b, hh, t: (0, hh)),
                  pl.BlockSpec((1, gv), lambda b, hh, t: (0, hh))],
        out_specs=pl.BlockSpec((tb, gv), lambda b, hh, t: (b * nt + t, hh)),
        out_shape=jax.ShapeDtypeStruct((batch * seq, val), BF16),
        scratch_shapes=[pltpu.VMEM((g, dv, dk), F32)],
        compiler_params=_params("arbitrary", "arbitrary", "arbitrary"),
        name="gla",
    )(u_qkv, u_qkv, u_qkv, u_rg, u_rg, gate_w2, row(gate_b), row(r_b), row(gn_g), row(gn_b))


def _ffn_up_kernel(a_ref, wg_ref, wu_ref, cw_ref, cb_ref, o_ref, carry_ref, wgbf_ref, wubf_ref,
                   *, seq):
    i = pl.program_id(1)
    tm = a_ref.shape[0]

    @pl.when(i == 0)
    def _():
        wgbf_ref[...] = wg_ref[...].astype(BF16)
        wubf_ref[...] = wu_ref[...].astype(BF16)

    a = a_ref[...]
    gate = jnp.dot(a, wgbf_ref[...], preferred_element_type=F32)
    up = jnp.dot(a, wubf_ref[...], preferred_element_type=F32)
    first = (i * tm) % seq == 0
    tail = carry_ref[...]
    carry_ref[...] = gate[tm - SUBLANES:, :]
    p1 = jnp.where(first, 0.0, tail[SUBLANES - 1:SUBLANES, :])
    p2 = jnp.where(first, 0.0, tail[SUBLANES - 2:SUBLANES - 1, :])
    row = lax.broadcasted_iota(jnp.int32, (tm, 1), 0)
    g1 = jnp.where(row == 0, p1, pltpu.roll(gate, 1, axis=0))
    g2 = jnp.where(row == 0, p2, jnp.where(row == 1, p1, pltpu.roll(gate, 2, axis=0)))
    conv = cw_ref[0:1, :] * g2 + cw_ref[1:2, :] * g1 + cw_ref[2:3, :] * gate + cb_ref[...]
    act = 0.5 * conv * (1.0 + lax.erf(conv * math.sqrt(0.5)))
    o_ref[...] = (act * up).astype(o_ref.dtype)


def _ffn_up(a, w_up3, layer, conv_w3, conv_b2, seq):
    m, k = a.shape
    dff = conv_b2.shape[1]
    tm, tn = FFN_TM, FFN_TN
    nj = dff // tn
    assert m % tm == 0 and dff % tn == 0 and seq % tm == 0 and conv_w3.shape[1] == CONV_W
    assert w_up3.shape[2] == 2 * dff
    return pl.pallas_call(
        functools.partial(_ffn_up_kernel, seq=seq),
        grid=(nj, m // tm),
        in_specs=[pl.BlockSpec((tm, k), lambda j, i: (i, 0)),
                  pl.BlockSpec((None, k, tn), lambda j, i: (layer, 0, j)),
                  pl.BlockSpec((None, k, tn), lambda j, i: (layer, 0, nj + j)),
                  pl.BlockSpec((None, CONV_W, tn), lambda j, i: (layer, 0, j)),
                  pl.BlockSpec((None, 1, tn), lambda j, i: (layer, 0, j))],
        out_specs=pl.BlockSpec((tm, tn), lambda j, i: (i, j)),
        out_shape=jax.ShapeDtypeStruct((m, dff), BF16),
        scratch_shapes=[pltpu.VMEM((SUBLANES, tn), F32), pltpu.VMEM((k, tn), BF16),
                        pltpu.VMEM((k, tn), BF16)],
        compiler_params=_params("arbitrary", "arbitrary"),
        name="ffn_up",
    )(a, w_up3, w_up3, conv_w3, conv_b2.reshape(conv_b2.shape[0], 1, dff))


def _pad_cols(w, n):
    return jnp.pad(w, ((0, 0), (0, n - w.shape[1])))


def _pad_rows(w, n):
    return jnp.pad(w, ((0, n - w.shape[0]), (0, 0)))


def _pad_vec(w, n):
    return jnp.pad(w, (0, n - w.shape[0]))


def kernel(x, even_w_in, even_shift_mu, rw_w0, rw_w2, rw_a0, rw_a2, rw_g2, rw_k_k, rw_k_a, rw_r_k, rw_gn_g, rw_gn_b, even_w_out, odd_w_in, gla_gate_w2, gla_gate_b, gla_r_b, gla_gn_g, gla_gn_b, odd_w_out, ln_mix_g, ln_mix_b, ln_ffn_g, ln_ffn_b, ffn_w_up, ffn_conv_w, ffn_conv_b, ffn_w_down):
    batch, seq, d = x.shape
    depth = ln_mix_g.shape[0]
    alpha = (2 * depth) ** 0.25
    m = batch * seq
    h = x.reshape(m, d)
    h_bf = h.astype(BF16)

    sb_w = SB_HEADS * SB_HEAD_DIM
    rw_w = rw_w0.shape[1]
    lw, la, lg = (_round_up(n, LANES) for n in (RW_DECAY_LORA, RW_AAA_LORA, RW_GATE_LORA))
    tn = MM_TN

    for layer in range(depth):
        i = layer // 2
        if layer % 2 == 0:
            c0 = 3 * sb_w
            c1 = c0 + 3 * rw_w
            c2 = c1 + RW_DECAY_LORA
            c3 = c2 + RW_AAA_LORA
            assert c0 % tn == 0 and c1 % tn == 0
            w_lora = jnp.concatenate([_pad_cols(even_w_in[i, :, c1:c2], lw),
                                      _pad_cols(even_w_in[i, :, c2:c3], la),
                                      _pad_cols(even_w_in[i, :, c3:], lg)], axis=1)
            w_lora = _pad_cols(w_lora, _round_up(w_lora.shape[1], tn))[None]
            mu = even_shift_mu[i]
            m0 = 3 * rw_w
            m1 = m0 + RW_DECAY_LORA
            m2 = m1 + RW_AAA_LORA
            mu_lora = _pad_vec(jnp.concatenate([_pad_vec(mu[m0:m1], lw), _pad_vec(mu[m1:m2], la),
                                                _pad_vec(mu[m2:], lg)]), w_lora.shape[2])
            u_sb = _matmul(h_bf, even_w_in, i, 0, c0, BF16, "even_in_sb")
            u_rkv = _matmul(h_bf, even_w_in, i, c0 // tn, c1 - c0, F32, "even_in_rkv")
            u_lora = _matmul(h_bf, w_lora, 0, 0, w_lora.shape[2], F32, "even_in_lora")
            o_sb = _sb_attention(u_sb, batch, seq)
            o_rw = _rwkv(u_rkv, u_lora, batch, seq, mu[:m0], mu_lora, rw_w0[i],
                         _pad_rows(rw_w2[i], lw).astype(BF16), rw_a0[i],
                         _pad_rows(rw_a2[i], la).astype(BF16),
                         _pad_rows(rw_g2[i], w_lora.shape[2] - lw - la).astype(BF16),
                         rw_k_k[i], rw_k_a[i], rw_r_k[i].reshape(-1), rw_gn_g[i], rw_gn_b[i])
            h, h_bf = _proj_ln([o_sb, o_rw], even_w_out, i, h, ln_mix_g[layer], ln_mix_b[layer],
                               alpha, LN_TM, "even_out_ln")
        else:
            key = gla_gate_b.shape[1]
            val = gla_r_b.shape[1]
            c0 = 2 * key + val
            c1 = c0 + GLA_GATE_RANK
            rank_p = _round_up(GLA_GATE_RANK, LANES)
            assert c0 % tn == 0
            w_rg = jnp.concatenate([odd_w_in[i, :, c1:], _pad_cols(odd_w_in[i, :, c0:c1], tn)],
                                   axis=1)[None]
            u_qkv = _matmul(h_bf, odd_w_in, i, 0, c0, F32, "odd_in_qkv")
            u_rg = _matmul(h_bf, w_rg, 0, 0, w_rg.shape[2], F32, "odd_in_rg")
            o = _gla(u_qkv, u_rg, batch, seq, _pad_rows(gla_gate_w2[i], rank_p), gla_gate_b[i],
                     gla_r_b[i], gla_gn_g[i], gla_gn_b[i])
            h, h_bf = _proj_ln([o], odd_w_out, i, h, ln_mix_g[layer], ln_mix_b[layer], alpha,
                               LN_TM, "odd_out_ln")
        mid = _ffn_up(h_bf, ffn_w_up, layer, ffn_conv_w, ffn_conv_b, seq)
        h, h_bf = _proj_ln([mid], ffn_w_down, layer, h, ln_ffn_g[layer], ln_ffn_b[layer], alpha,
                           DOWN_TM, "ffn_down_ln")
    return h.reshape(batch, seq, d)
```

```python
import functools
import math

import jax
import jax.numpy as jnp
from jax import lax
from jax.experimental import pallas as pl
from jax.experimental.pallas import tpu as pltpu

F32 = jnp.float32
BF16 = jnp.bfloat16

SB_HEADS = 8
SB_HEAD_DIM = 128
RW_HEAD_DIM = 64
RW_DECAY_LORA = 64
RW_AAA_LORA = 64
RW_GATE_LORA = 160
RW_GN_EPS = 64e-5
GLA_HEADS = 4
GLA_GATE_RANK = 16
GLA_TAU = 16.0
CONV_W = 3
LN_EPS = 1e-5

LANES = 128
SUBLANES = 8
VMEM_LIMIT_BYTES = 56 * 1024 * 1024

MM_TM = 1024
MM_TN = 1024
LORA_TN = 512
LN_TM = 512
LN_SUB = 2
FFN_TM = 1024
FFN_TN = 512
DOWN_TM = 256
CAST_ROWS = 256
SB_TQ = 256
SB_HEADS_PER_STEP = 4
RW_CHUNK = 64
RW_PAIR = LANES // RW_HEAD_DIM
RW_BATCH = 2
GLA_CHUNK = 64
GLA_TB = 256
GLA_HEADS_PER_STEP = 2


def _round_up(n, m):
    return (n + m - 1) // m * m


def _params(*sem):
    return pltpu.CompilerParams(dimension_semantics=sem, vmem_limit_bytes=VMEM_LIMIT_BYTES)


def _log_sigmoid(x):
    return jnp.minimum(x, 0.0) - jnp.log(1.0 + jnp.exp(-jnp.abs(x)))


def _sigmoid(x):
    return 1.0 / (1.0 + jnp.exp(-x))


def _dot(a, b):
    return jnp.dot(a.astype(BF16), b.astype(BF16), preferred_element_type=F32)


def _dot_nt(a, b):
    return lax.dot_general(a.astype(BF16), b.astype(BF16), (((1,), (1,)), ((), ())),
                           preferred_element_type=F32)


def _dot_tn(a, b):
    return lax.dot_general(a.astype(BF16), b.astype(BF16), (((0,), (0,)), ((), ())),
                           preferred_element_type=F32)


def _split2(x):
    hi = x.astype(BF16)
    return hi, (x - hi.astype(F32)).astype(BF16)


def _split3(x):
    hi = x.astype(BF16)
    r = x - hi.astype(F32)
    mid = r.astype(BF16)
    return hi, mid, (r - mid.astype(F32)).astype(BF16)


def _dot_hilo(x, w01):
    hi, lo = _split2(x)
    return (jnp.dot(hi, w01, preferred_element_type=F32)
            + jnp.dot(lo, w01, preferred_element_type=F32))


def _cumsum_dot(w01, x):
    return sum(jnp.dot(w01, p, preferred_element_type=F32) for p in _split3(x))


def _dot_split(x, w):
    xh, xl = _split2(x)
    wh, wl = _split2(w)
    return (jnp.dot(xh, wh, preferred_element_type=F32) + jnp.dot(xh, wl, preferred_element_type=F32)
            + jnp.dot(xl, wh, preferred_element_type=F32))


def _layer_norm(y, g, b, eps):
    mu = jnp.mean(y, axis=-1, keepdims=True)
    d = y - mu
    var = jnp.mean(d * d, axis=-1, keepdims=True)
    return d * lax.rsqrt(var + eps) * g + b


def _mm_kernel(a_ref, wt_ref, o_ref, wbf_ref):
    @pl.when(pl.program_id(1) == 0)
    def _():
        wbf_ref[...] = wt_ref[...].astype(BF16)

    o_ref[...] = lax.dot_general(a_ref[...], wbf_ref[...], (((1,), (1,)), ((), ())),
                                 preferred_element_type=F32).astype(o_ref.dtype)


def _matmul(a, wt3, layer, tile0, n_out, out_dtype, tn, name):
    m, k = a.shape
    tm = MM_TM
    assert m % tm == 0 and n_out % tn == 0 and wt3.shape[2] == k
    return pl.pallas_call(
        _mm_kernel,
        grid=(n_out // tn, m // tm),
        in_specs=[pl.BlockSpec((tm, k), lambda j, i: (i, 0)),
                  pl.BlockSpec((None, tn, k), lambda j, i: (layer, tile0 + j, 0))],
        out_specs=pl.BlockSpec((tm, tn), lambda j, i: (i, j)),
        out_shape=jax.ShapeDtypeStruct((m, n_out), out_dtype),
        scratch_shapes=[pltpu.VMEM((tn, k), BF16)],
        compiler_params=_params("arbitrary", "arbitrary"),
        name=name,
    )(a, wt3)


def _proj_ln_kernel(*refs, n_parts, n_cast, n_sub, alpha):
    a_refs = refs[:n_parts]
    w_ref, res_ref, g_ref, b_ref, o_ref, obf_ref, wbf_ref = refs[n_parts:]
    s = pl.program_id(0)
    rows = w_ref.shape[0]

    @pl.when(s < n_cast)
    def _():
        r0 = pl.multiple_of(s * rows, rows)
        wbf_ref[pl.ds(r0, rows), :] = w_ref[...].astype(BF16)

    @pl.when(s >= n_cast)
    def _():
        sub = res_ref.shape[0] // n_sub
        accs = []
        for t in range(n_sub):
            rs = slice(t * sub, (t + 1) * sub)
            acc = None
            k0 = 0
            for a_ref in a_refs:
                kp = a_ref.shape[1]
                part = jnp.dot(a_ref[rs, :], wbf_ref[k0:k0 + kp, :], preferred_element_type=F32)
                acc = part if acc is None else acc + part
                k0 += kp
            accs.append(acc)
        for t in range(n_sub):
            rs = slice(t * sub, (t + 1) * sub)
            y = alpha * res_ref[rs, :] + accs[t]
            out = _layer_norm(y, g_ref[...], b_ref[...], LN_EPS)
            o_ref[rs, :] = out
            obf_ref[rs, :] = out.astype(BF16)


def _proj_ln(a_parts, w3, layer, res, g, b, alpha, tm, n_sub, name):
    m, d = res.shape
    k = w3.shape[1]
    rows = CAST_ROWS
    n_cast = k // rows
    assert m % tm == 0 and k % rows == 0 and sum(a.shape[1] for a in a_parts) == k
    tok = lambda s: (jnp.maximum(s - n_cast, 0), 0)
    in_specs = [pl.BlockSpec((tm, a.shape[1]), tok) for a in a_parts]
    in_specs += [pl.BlockSpec((None, rows, d), lambda s: (layer, jnp.minimum(s, n_cast - 1), 0)),
                 pl.BlockSpec((tm, d), tok),
                 pl.BlockSpec((1, d), lambda s: (0, 0)),
                 pl.BlockSpec((1, d), lambda s: (0, 0))]
    return pl.pallas_call(
        functools.partial(_proj_ln_kernel, n_parts=len(a_parts), n_cast=n_cast, n_sub=n_sub,
                          alpha=alpha),
        grid=(n_cast + m // tm,),
        in_specs=in_specs,
        out_specs=[pl.BlockSpec((tm, d), tok), pl.BlockSpec((tm, d), tok)],
        out_shape=[jax.ShapeDtypeStruct((m, d), F32), jax.ShapeDtypeStruct((m, d), BF16)],
        scratch_shapes=[pltpu.VMEM((k, d), BF16)],
        compiler_params=_params("arbitrary"),
        name=name,
    )(*a_parts, w3, res, g.reshape(1, d), b.reshape(1, d))


def _sb_kernel(q_ref, k_ref, v_ref, o_ref, *, tq, dh, scale):
    i = pl.program_id(2)
    n_h = q_ref.shape[1] // dh
    heads = range(n_h)
    cols = [slice(hh * dh, (hh + 1) * dh) for hh in heads]
    row = lax.broadcasted_iota(jnp.int32, (tq, tq), 0)
    col = lax.broadcasted_iota(jnp.int32, (tq, tq), 1)
    after = jnp.where(row > col, 1.0, 0.0).astype(BF16)
    causal = col < row

    def tile(j, state, diagonal):
        k0 = pl.multiple_of(j * tq, tq)
        z = [lax.dot_general(q_ref[:, cols[hh]], k_ref[pl.ds(k0, tq), cols[hh]],
                             (((1,), (1,)), ((), ())), preferred_element_type=F32) * scale
             for hh in heads]
        ls = [_log_sigmoid(z[hh]) for hh in heads]
        lk = [ls[hh] - z[hh] for hh in heads]
        if diagonal:
            lk = [jnp.where(causal, lk[hh], 0.0) for hh in heads]
        suffix = [_dot_hilo(lk[hh], after) + state[hh][0] for hh in heads]
        w = [jnp.exp(ls[hh] + suffix[hh]) for hh in heads]
        if diagonal:
            w = [jnp.where(causal, w[hh], 0.0) for hh in heads]
        acc = [state[hh][1] + jnp.dot(w[hh].astype(BF16), v_ref[pl.ds(k0, tq), cols[hh]],
                                      preferred_element_type=F32) for hh in heads]
        carry = [suffix[hh][:, 0:1] + lk[hh][:, 0:1] for hh in heads]
        return tuple((carry[hh], acc[hh]) for hh in heads)

    init = tuple((jnp.zeros((tq, 1), F32), jnp.zeros((tq, dh), F32)) for _ in heads)
    state = tile(i, init, True)
    state = lax.fori_loop(0, i, lambda n, st: tile(i - 1 - n, st, False), state)
    for hh in heads:
        o_ref[:, cols[hh]] = state[hh][1].astype(o_ref.dtype)


def _sb_attention(u_sb, batch, seq):
    tq = SB_TQ
    nq = seq // tq
    dh = SB_HEAD_DIM
    g = SB_HEADS_PER_STEP
    ng = SB_HEADS // g
    wd = g * dh
    assert seq % tq == 0 and dh % LANES == 0 and SB_HEADS % g == 0
    return pl.pallas_call(
        functools.partial(_sb_kernel, tq=tq, dh=dh, scale=dh ** -0.5),
        grid=(batch, ng, nq),
        in_specs=[pl.BlockSpec((tq, wd), lambda b, hh, i: (b * nq + i, hh)),
                  pl.BlockSpec((seq, wd), lambda b, hh, i: (b, ng + hh)),
                  pl.BlockSpec((seq, wd), lambda b, hh, i: (b, 2 * ng + hh))],
        out_specs=pl.BlockSpec((tq, wd), lambda b, hh, i: (b * nq + i, hh)),
        out_shape=jax.ShapeDtypeStruct((batch * seq, SB_HEADS * dh), BF16),
        compiler_params=_params("arbitrary", "arbitrary", "arbitrary"),
        name="sb_attention",
    )(u_sb, u_sb, u_sb)


def _token_shift(u, prev_ref, mu):
    rows = u.shape[0]
    row1 = lax.broadcasted_iota(jnp.int32, (rows, 1), 0)
    prev = jnp.where(row1 == 0, prev_ref[SUBLANES - 1:SUBLANES, :], pltpu.roll(u, 1, axis=0))
    prev_ref[...] = u[rows - SUBLANES:, :]
    return u + (prev - u) * mu


def _rwkv_kernel(u_ref, ul_ref, mu_ref, mul_ref, w0_ref, w2_ref, a0_ref, a2_ref, g2_ref, kk_ref,
                 ka_ref, rk_ref, gng_ref, gnb_ref, o_ref, state_ref, prev_ref, prevl_ref,
                 *, chunk, width):
    L, C = chunk, width
    nb = u_ref.shape[0]
    n_pairs = C // LANES
    N = LANES // RW_PAIR
    R = nb * L
    c = pl.program_id(1)

    @pl.when(c == 0)
    def _():
        state_ref[...] = jnp.zeros_like(state_ref)
        prev_ref[...] = jnp.zeros_like(prev_ref)
        prevl_ref[...] = jnp.zeros_like(prevl_ref)

    us = jnp.concatenate([_token_shift(u_ref[bb], prev_ref.at[bb], mu_ref[...])
                          for bb in range(nb)], axis=0)
    ul = jnp.concatenate([_token_shift(ul_ref[bb], prevl_ref.at[bb], mul_ref[...])
                          for bb in range(nb)], axis=0)
    r = us[:, 0:C]
    k = us[:, C:2 * C]
    v = us[:, 2 * C:3 * C]
    o1 = _round_up(RW_DECAY_LORA, LANES)
    o2 = o1 + _round_up(RW_AAA_LORA, LANES)
    dw = ul[:, :o1]
    da = ul[:, o1:o2]
    dg = ul[:, o2:]

    w_log = _log_sigmoid(w0_ref[...] + _dot(jnp.tanh(dw), w2_ref[...])) - 0.5
    log_decay = -jnp.exp(w_log)
    a = _sigmoid(a0_ref[...] + _dot(da, a2_ref[...]))
    gate = _dot(_sigmoid(dg), g2_ref[...])

    rowl = lax.broadcasted_iota(jnp.int32, (R, R), 0)
    coll = lax.broadcasted_iota(jnp.int32, (R, R), 1)
    lower01 = jnp.where(coll <= rowl, jnp.where(coll >= (rowl // L) * L, 1.0, 0.0),
                        0.0).astype(BF16)
    lc = _cumsum_dot(lower01, log_decay)
    g_incl = jnp.exp(lc)
    g_excl = jnp.exp(lc - log_decay)
    g_inv = jnp.exp(-lc)
    g_last = [g_incl[(bb + 1) * L - 1:(bb + 1) * L, :] for bb in range(nb)]
    g_last_rows = jnp.concatenate([jnp.broadcast_to(g_last[bb], (L, C)) for bb in range(nb)],
                                  axis=0)

    lane = lax.broadcasted_iota(jnp.int32, (1, LANES), 1)
    first = lane < N

    def head_sum(x):
        out = []
        for p in range(n_pairs):
            xs = x[:, p * LANES:(p + 1) * LANES]
            s1 = jnp.sum(jnp.where(first, xs, 0.0), axis=-1, keepdims=True)
            s2 = jnp.sum(jnp.where(first, 0.0, xs), axis=-1, keepdims=True)
            out.append(jnp.where(first, s1, s2))
        return jnp.concatenate(out, axis=1)

    kkf = k * kk_ref[...]
    kk = kkf * lax.rsqrt(jnp.maximum(head_sum(kkf * kkf), 1e-24))
    kmod = k * (1.0 + (a - 1.0) * ka_ref[...])
    at_all = -(kk * g_excl)
    bt_all = kk * a * g_inv
    kt_all = kmod * g_inv
    rt_all = r * g_incl
    be_all = bt_all * g_last_rows
    ke_all = kt_all * g_last_rows
    bonus = head_sum(r * kmod * rk_ref[...]) * v

    t_idx = lax.broadcasted_iota(jnp.int32, (L, LANES), 0)
    j_idx = lax.broadcasted_iota(jnp.int32, (L, LANES), 1)
    strict = t_idx > j_idx % N
    incl = t_idx >= j_idx % N
    keep1 = jnp.where(j_idx < N, 1.0, 0.0).astype(BF16)
    keep2 = jnp.where(j_idx < N, 0.0, 1.0).astype(BF16)

    def bd(x):
        xb = x.astype(BF16)
        return jnp.concatenate([xb * keep1, xb * keep2], axis=0)

    units = [(bb, p) for bb in range(nb) for p in range(n_pairs)]
    nu = range(len(units))
    rs = [slice(bb * L, (bb + 1) * L) for bb, _ in units]
    sl = [slice(p * LANES, (p + 1) * LANES) for _, p in units]
    lhs = [jnp.concatenate([at_all[rs[u], sl[u]], rt_all[rs[u], sl[u]]], axis=0) for u in nu]
    s0 = [state_ref[bb, p] for bb, p in units]
    g = [_dot_nt(lhs[u], jnp.concatenate([bd(bt_all[rs[u], sl[u]]), bd(kt_all[rs[u], sl[u]])],
                                         axis=0)) for u in nu]
    ss = [_dot_nt(lhs[u], bd(s0[u])) for u in nu]
    pw = [jnp.where(strict, g[u][:L, :LANES], 0.0) for u in nu]
    a_ak = [jnp.where(strict, g[u][:L, LANES:], 0.0) for u in nu]
    a_rb = [jnp.where(incl, g[u][L:, :LANES], 0.0) for u in nu]
    a_rk = [jnp.where(incl, g[u][L:, LANES:], 0.0) for u in nu]
    av = [_dot(jnp.concatenate([a_ak[u], a_rk[u]], axis=0), bd(v[rs[u], sl[u]])) for u in nu]
    nxt = [_dot(pw[u], bd(pw[u])) for u in nu]
    x = [ss[u][:L] + av[u][:L] for u in nu]
    span = 1
    while True:
        x = [x[u] + _dot(pw[u], bd(x[u])) for u in nu]
        if 2 * span >= L:
            break
        pw = nxt
        span *= 2
        if 2 * span < L:
            nxt = [_dot(pw[u], bd(pw[u])) for u in nu]
    y = [ss[u][L:] + av[u][L:] + _dot(a_rb[u], bd(x[u])) for u in nu]
    for u, (bb, p) in enumerate(units):
        full = _dot_tn(jnp.concatenate([x[u], v[rs[u], sl[u]]], axis=0),
                       jnp.concatenate([be_all[rs[u], sl[u]], ke_all[rs[u], sl[u]]], axis=0))
        state_ref[bb, p] = s0[u] * g_last[bb][:, sl[u]] + jnp.where(first, full[:N], full[N:])

    y = jnp.concatenate([jnp.concatenate(y[bb * n_pairs:(bb + 1) * n_pairs], axis=1)
                         for bb in range(nb)], axis=0)
    d = y - head_sum(y) * (1.0 / N)
    var = head_sum(d * d) * (1.0 / N)
    out = (d * lax.rsqrt(var + RW_GN_EPS) * gng_ref[...] + gnb_ref[...] + bonus) * gate
    for bb in range(nb):
        o_ref[bb] = out[bb * L:(bb + 1) * L, :].astype(o_ref.dtype)


def _rwkv(u_rkv, u_lora, batch, seq, mu, mu_lora, w0, w2, a0, a2, g2, k_k, k_a, r_k, gn_g, gn_b):
    L = RW_CHUNK
    nb = RW_BATCH
    width = w0.shape[0]
    n = RW_HEAD_DIM
    nc = seq // L
    wu = u_rkv.shape[1]
    wl = u_lora.shape[1]
    assert seq % L == 0 and L % SUBLANES == 0 and width % LANES == 0 and L == n
    assert wu == 3 * width and batch % nb == 0
    row = lambda x: x.reshape(1, -1)
    const = lambda shape: pl.BlockSpec(shape, lambda b, c: (0, 0))
    out = pl.pallas_call(
        functools.partial(_rwkv_kernel, chunk=L, width=width),
        grid=(batch // nb, nc),
        in_specs=[pl.BlockSpec((nb, L, wu), lambda b, c: (b, c, 0)),
                  pl.BlockSpec((nb, L, wl), lambda b, c: (b, c, 0)),
                  const((1, wu)), const((1, wl)), const((1, width)), const(w2.shape),
                  const((1, width)), const(a2.shape), const(g2.shape), const((1, width)),
                  const((1, width)), const((1, width)), const((1, width)), const((1, width))],
        out_specs=pl.BlockSpec((nb, L, width), lambda b, c: (b, c, 0)),
        out_shape=jax.ShapeDtypeStruct((batch, seq, width), BF16),
        scratch_shapes=[pltpu.VMEM((nb, width // LANES, n, LANES), F32),
                        pltpu.VMEM((nb, SUBLANES, wu), F32), pltpu.VMEM((nb, SUBLANES, wl), F32)],
        compiler_params=_params("arbitrary", "arbitrary"),
        name="rwkv7",
    )(u_rkv.reshape(batch, seq, wu), u_lora.reshape(batch, seq, wl), row(mu), row(mu_lora),
      row(w0), w2, row(a0), a2, g2, row(k_k), row(k_a), row(r_k), row(gn_g), row(gn_b))
    return out.reshape(batch * seq, width)


def _gla_kernel(q_ref, k_ref, v_ref, r_ref, h_ref, wdg_ref, gw_ref, gb_ref, rb_ref, gng_ref,
                gnb_ref, o_ref, state_ref, *, chunk, n_h, scale):
    L = chunk
    tb = q_ref.shape[0]
    dk = q_ref.shape[1] // n_h
    dv = v_ref.shape[1] // n_h
    n_ch = tb // L
    heads = range(n_h)
    chunks = range(n_ch)
    kc = [slice(hh * dk, (hh + 1) * dk) for hh in heads]
    vc = [slice(hh * dv, (hh + 1) * dv) for hh in heads]
    rc = [slice(c * L, (c + 1) * L) for c in chunks]
    t = pl.program_id(2)

    @pl.when(t == 0)
    def _():
        state_ref[...] = jnp.zeros_like(state_ref)

    row = lax.broadcasted_iota(jnp.int32, (tb, tb), 0)
    col = lax.broadcasted_iota(jnp.int32, (tb, tb), 1)
    chunk_start = (row // L) * L
    lower = jnp.where(col <= row, jnp.where(col >= chunk_start, 1.0, 0.0), 0.0)
    lower01 = lower.astype(BF16)
    in_chunk = lower > 0.0

    dg = jnp.dot(h_ref[...], wdg_ref[...], preferred_element_type=F32)
    log_a = _log_sigmoid(_dot_split(dg, gw_ref[...]) + gb_ref[...]) / GLA_TAU
    b = _cumsum_dot(lower01, log_a)
    eb = jnp.exp(b)
    enb = jnp.exp(-b)
    decay = [eb[(c + 1) * L - 1:(c + 1) * L, :] for c in chunks]
    eb_last = jnp.concatenate([jnp.broadcast_to(decay[c], (L, n_h * dk)) for c in chunks], axis=0)
    q_dec = (q_ref[...] * scale * eb).astype(BF16)
    k_inv = k_ref[...] * enb
    k_end = (k_inv * eb_last).astype(BF16)
    k_inv = k_inv.astype(BF16)
    v = v_ref[...].astype(BF16)
    att = [jnp.where(in_chunk, _dot_nt(q_dec[:, kc[hh]], k_inv[:, kc[hh]]), 0.0) for hh in heads]
    o = [_dot(att[hh], v[:, vc[hh]]) for hh in heads]
    kv = [[_dot_tn(v[rc[c], vc[hh]], k_end[rc[c], kc[hh]]) for c in chunks] for hh in heads]
    st = [state_ref[hh] for hh in heads]
    inter = [[] for _ in heads]
    for c in chunks:
        for hh in heads:
            inter[hh].append(_dot_nt(q_dec[rc[c], kc[hh]], st[hh]))
            st[hh] = st[hh] * decay[c][:, kc[hh]] + kv[hh][c]
    for hh in heads:
        state_ref[hh] = st[hh]
        oh = o[hh] + jnp.concatenate(inter[hh], axis=0)
        oh = _layer_norm(oh, gng_ref[:, vc[hh]], gnb_ref[:, vc[hh]], LN_EPS)
        x = r_ref[:, vc[hh]] + rb_ref[:, vc[hh]]
        o_ref[:, vc[hh]] = (oh * (x * _sigmoid(x))).astype(o_ref.dtype)


def _gla(u_qkv, u_r, h_bf, w_dg, batch, seq, gate_w2, gate_b, r_b, gn_g, gn_b):
    tb = GLA_TB
    g = GLA_HEADS_PER_STEP
    key = gate_b.shape[0]
    val = r_b.shape[0]
    dk, dv = key // GLA_HEADS, val // GLA_HEADS
    gk, gv = g * dk, g * dv
    ng = GLA_HEADS // g
    nt = seq // tb
    rank = gate_w2.shape[0]
    assert seq % tb == 0 and tb % GLA_CHUNK == 0 and GLA_HEADS % g == 0
    assert (2 * key) % gv == 0 and w_dg.shape[1] == rank
    d = h_bf.shape[1]
    row = lambda x: x.reshape(1, -1)
    return pl.pallas_call(
        functools.partial(_gla_kernel, chunk=GLA_CHUNK, n_h=g, scale=dk ** -0.5),
        grid=(batch, ng, nt),
        in_specs=[pl.BlockSpec((tb, gk), lambda b, hh, t: (b * nt + t, hh)),
                  pl.BlockSpec((tb, gk), lambda b, hh, t: (b * nt + t, ng + hh)),
                  pl.BlockSpec((tb, gv), lambda b, hh, t: (b * nt + t, 2 * key // gv + hh)),
                  pl.BlockSpec((tb, gv), lambda b, hh, t: (b * nt + t, hh)),
                  pl.BlockSpec((tb, d), lambda b, hh, t: (b * nt + t, 0)),
                  pl.BlockSpec((d, rank), lambda b, hh, t: (0, 0)),
                  pl.BlockSpec((rank, gk), lambda b, hh, t: (0, hh)),
                  pl.BlockSpec((1, gk), lambda b, hh, t: (0, hh)),
                  pl.BlockSpec((1, gv), lambda b, hh, t: (0, hh)),
                  pl.BlockSpec((1, gv), lambda b, hh, t: (0, hh)),
                  pl.BlockSpec((1, gv), lambda b, hh, t: (0, hh))],
        out_specs=pl.BlockSpec((tb, gv), lambda b, hh, t: (b * nt + t, hh)),
        out_shape=jax.ShapeDtypeStruct((batch * seq, val), BF16),
        scratch_shapes=[pltpu.VMEM((g, dv, dk), F32)],
        compiler_params=_params("arbitrary", "arbitrary", "arbitrary"),
        name="gla",
    )(u_qkv, u_qkv, u_qkv, u_r, h_bf, w_dg, gate_w2, row(gate_b), row(r_b), row(gn_g), row(gn_b))


def _ffn_up_kernel(a_ref, wg_ref, wu_ref, cw_ref, cb_ref, o_ref, carry_ref, wgbf_ref, wubf_ref,
                   *, seq):
    i = pl.program_id(1)
    tm = a_ref.shape[0]

    @pl.when(i == 0)
    def _():
        wgbf_ref[...] = wg_ref[...].astype(BF16)
        wubf_ref[...] = wu_ref[...].astype(BF16)

    a = a_ref[...]
    gate = jnp.dot(a, wgbf_ref[...], preferred_element_type=F32)
    up = jnp.dot(a, wubf_ref[...], preferred_element_type=F32)
    first = (i * tm) % seq == 0
    tail = carry_ref[...]
    carry_ref[...] = gate[tm - SUBLANES:, :]
    p1 = jnp.where(first, 0.0, tail[SUBLANES - 1:SUBLANES, :])
    p2 = jnp.where(first, 0.0, tail[SUBLANES - 2:SUBLANES - 1, :])
    row = lax.broadcasted_iota(jnp.int32, (tm, 1), 0)
    g1 = jnp.where(row == 0, p1, pltpu.roll(gate, 1, axis=0))
    g2 = jnp.where(row == 0, p2, jnp.where(row == 1, p1, pltpu.roll(gate, 2, axis=0)))
    conv = cw_ref[0:1, :] * g2 + cw_ref[1:2, :] * g1 + cw_ref[2:3, :] * gate + cb_ref[...]
    act = 0.5 * conv * (1.0 + lax.erf(conv * math.sqrt(0.5)))
    o_ref[...] = (act * up).astype(o_ref.dtype)


def _ffn_up(a, w_up3, layer, conv_w3, conv_b2, seq):
    m, k = a.shape
    dff = conv_b2.shape[1]
    tm, tn = FFN_TM, FFN_TN
    nj = dff // tn
    assert m % tm == 0 and dff % tn == 0 and seq % tm == 0 and conv_w3.shape[1] == CONV_W
    assert w_up3.shape[2] == 2 * dff
    return pl.pallas_call(
        functools.partial(_ffn_up_kernel, seq=seq),
        grid=(nj, m // tm),
        in_specs=[pl.BlockSpec((tm, k), lambda j, i: (i, 0)),
                  pl.BlockSpec((None, k, tn), lambda j, i: (layer, 0, j)),
                  pl.BlockSpec((None, k, tn), lambda j, i: (layer, 0, nj + j)),
                  pl.BlockSpec((None, CONV_W, tn), lambda j, i: (layer, 0, j)),
                  pl.BlockSpec((None, 1, tn), lambda j, i: (layer, 0, j))],
        out_specs=pl.BlockSpec((tm, tn), lambda j, i: (i, j)),
        out_shape=jax.ShapeDtypeStruct((m, dff), BF16),
        scratch_shapes=[pltpu.VMEM((SUBLANES, tn), F32), pltpu.VMEM((k, tn), BF16),
                        pltpu.VMEM((k, tn), BF16)],
        compiler_params=_params("arbitrary", "arbitrary"),
        name="ffn_up",
    )(a, w_up3, w_up3, conv_w3, conv_b2.reshape(conv_b2.shape[0], 1, dff))


def _pad_cols(w, n):
    return jnp.pad(w, ((0, 0), (0, n - w.shape[1])))


def _pad_rows(w, n):
    return jnp.pad(w, ((0, n - w.shape[0]), (0, 0)))


def _pad_vec(w, n):
    return jnp.pad(w, (0, n - w.shape[0]))


def kernel(x, even_w_in, even_shift_mu, rw_w0, rw_w2, rw_a0, rw_a2, rw_g2, rw_k_k, rw_k_a, rw_r_k, rw_gn_g, rw_gn_b, even_w_out, odd_w_in, gla_gate_w2, gla_gate_b, gla_r_b, gla_gn_g, gla_gn_b, odd_w_out, ln_mix_g, ln_mix_b, ln_ffn_g, ln_ffn_b, ffn_w_up, ffn_conv_w, ffn_conv_b, ffn_w_down):
    batch, seq, d = x.shape
    depth = ln_mix_g.shape[0]
    alpha = (2 * depth) ** 0.25
    m = batch * seq
    h = x.reshape(m, d)
    h_bf = h.astype(BF16)

    sb_w = SB_HEADS * SB_HEAD_DIM
    rw_w = rw_w0.shape[1]
    lw, la, lg = (_round_up(n, LANES) for n in (RW_DECAY_LORA, RW_AAA_LORA, RW_GATE_LORA))
    tn = MM_TN
    even_wt = jnp.swapaxes(even_w_in, 1, 2)
    odd_wt = jnp.swapaxes(odd_w_in, 1, 2)

    for layer in range(depth):
        i = layer // 2
        if layer % 2 == 0:
            c0 = 3 * sb_w
            c1 = c0 + 3 * rw_w
            c2 = c1 + RW_DECAY_LORA
            c3 = c2 + RW_AAA_LORA
            assert c0 % tn == 0 and c1 % tn == 0
            wt_lora = jnp.concatenate([_pad_rows(even_wt[i, c1:c2], lw),
                                       _pad_rows(even_wt[i, c2:c3], la),
                                       _pad_rows(even_wt[i, c3:], lg)], axis=0)
            n_lora = _round_up(wt_lora.shape[0], LORA_TN)
            wt_lora = _pad_rows(wt_lora, n_lora)[None]
            mu = even_shift_mu[i]
            m0 = 3 * rw_w
            m1 = m0 + RW_DECAY_LORA
            m2 = m1 + RW_AAA_LORA
            mu_lora = _pad_vec(jnp.concatenate([_pad_vec(mu[m0:m1], lw), _pad_vec(mu[m1:m2], la),
                                                _pad_vec(mu[m2:], lg)]), n_lora)
            u_sb = _matmul(h_bf, even_wt, i, 0, c0, BF16, tn, "even_in_sb")
            u_rkv = _matmul(h_bf, even_wt, i, c0 // tn, c1 - c0, F32, tn, "even_in_rkv")
            u_lora = _matmul(h_bf, wt_lora, 0, 0, n_lora, F32, LORA_TN, "even_in_lora")
            o_sb = _sb_attention(u_sb, batch, seq)
            o_rw = _rwkv(u_rkv, u_lora, batch, seq, mu[:m0], mu_lora, rw_w0[i],
                         _pad_rows(rw_w2[i], lw).astype(BF16), rw_a0[i],
                         _pad_rows(rw_a2[i], la).astype(BF16),
                         _pad_rows(rw_g2[i], n_lora - lw - la).astype(BF16),
                         rw_k_k[i], rw_k_a[i], rw_r_k[i].reshape(-1), rw_gn_g[i], rw_gn_b[i])
            h, h_bf = _proj_ln([o_sb, o_rw], even_w_out, i, h, ln_mix_g[layer], ln_mix_b[layer],
                               alpha, LN_TM, LN_SUB, "even_out_ln")
        else:
            key = gla_gate_b.shape[1]
            val = gla_r_b.shape[1]
            c0 = 2 * key + val
            c1 = c0 + GLA_GATE_RANK
            rank_p = _round_up(GLA_GATE_RANK, LANES)
            assert c0 % tn == 0 and val % tn == 0
            w_dg = _pad_cols(odd_w_in[i, :, c0:c1], rank_p).astype(BF16)
            u_qkv = _matmul(h_bf, odd_wt, i, 0, c0, F32, tn, "odd_in_qkv")
            u_r = _matmul(h_bf, odd_wt[i, c1:][None], 0, 0, val, F32, tn, "odd_in_r")
            o = _gla(u_qkv, u_r, h_bf, w_dg, batch, seq, _pad_rows(gla_gate_w2[i], rank_p),
                     gla_gate_b[i], gla_r_b[i], gla_gn_g[i], gla_gn_b[i])
            h, h_bf = _proj_ln([o], odd_w_out, i, h, ln_mix_g[layer], ln_mix_b[layer], alpha,
                               LN_TM, LN_SUB, "odd_out_ln")
        mid = _ffn_up(h_bf, ffn_w_up, layer, ffn_conv_w, ffn_conv_b, seq)
        h, h_bf = _proj_ln([mid], ffn_w_down, layer, h, ln_ffn_g[layer], ln_ffn_b[layer], alpha,
                           DOWN_TM, 1, "ffn_down_ln")
    return h.reshape(batch, seq, d)
```

```python
import functools
import math

import jax
import jax.numpy as jnp
from jax import lax
from jax.experimental import pallas as pl
from jax.experimental.pallas import tpu as pltpu

F32 = jnp.float32
BF16 = jnp.bfloat16

SB_HEADS = 8
SB_HEAD_DIM = 128
RW_HEAD_DIM = 64
RW_DECAY_LORA = 64
RW_AAA_LORA = 64
RW_GATE_LORA = 160
RW_GN_EPS = 64e-5
GLA_HEADS = 4
GLA_GATE_RANK = 16
GLA_TAU = 16.0
CONV_W = 3
LN_EPS = 1e-5

LANES = 128
SUBLANES = 8
VMEM_LIMIT_BYTES = 56 * 1024 * 1024

MM_TM = 1024
MM_TN = 1024
LORA_TN = 512
LN_TM = 512
FFN_TM = 1024
FFN_TN = 512
DOWN_TM = 256
CAST_ROWS = 256
SB_TQ = 256
SB_HEADS_PER_STEP = 4
SB_LOG_UNDERFLOW = -105.0
RW_CHUNK = 64
RW_PAIR = LANES // RW_HEAD_DIM
RW_BATCH = 2
GLA_CHUNK = 64
GLA_TB = 256
GLA_HEADS_PER_STEP = 2


def _round_up(n, m):
    return (n + m - 1) // m * m


def _params(*sem):
    return pltpu.CompilerParams(dimension_semantics=sem, vmem_limit_bytes=VMEM_LIMIT_BYTES)


def _log_sigmoid(x):
    return jnp.minimum(x, 0.0) - jnp.log(1.0 + jnp.exp(-jnp.abs(x)))


def _sigmoid(x):
    return 1.0 / (1.0 + jnp.exp(-x))


def _dot(a, b):
    return jnp.dot(a.astype(BF16), b.astype(BF16), preferred_element_type=F32)


def _dot_nt(a, b):
    return lax.dot_general(a.astype(BF16), b.astype(BF16), (((1,), (1,)), ((), ())),
                           preferred_element_type=F32)


def _dot_tn(a, b):
    return lax.dot_general(a.astype(BF16), b.astype(BF16), (((0,), (0,)), ((), ())),
                           preferred_element_type=F32)


def _split2(x):
    hi = x.astype(BF16)
    return hi, (x - hi.astype(F32)).astype(BF16)


def _split3(x):
    hi = x.astype(BF16)
    r = x - hi.astype(F32)
    mid = r.astype(BF16)
    return hi, mid, (r - mid.astype(F32)).astype(BF16)


def _dot_hilo(x, w01):
    hi, lo = _split2(x)
    return (jnp.dot(hi, w01, preferred_element_type=F32)
            + jnp.dot(lo, w01, preferred_element_type=F32))


def _cumsum_dot(w01, x):
    return sum(jnp.dot(w01, p, preferred_element_type=F32) for p in _split3(x))


def _dot_split(x, w):
    xh, xl = _split2(x)
    wh, wl = _split2(w)
    return (jnp.dot(xh, wh, preferred_element_type=F32) + jnp.dot(xh, wl, preferred_element_type=F32)
            + jnp.dot(xl, wh, preferred_element_type=F32))


def _layer_norm(y, g, b, eps):
    mu = jnp.mean(y, axis=-1, keepdims=True)
    d = y - mu
    var = jnp.mean(d * d, axis=-1, keepdims=True)
    return d * lax.rsqrt(var + eps) * g + b


def _mm_kernel(a_ref, wt_ref, o_ref, wbf_ref):
    @pl.when(pl.program_id(1) == 0)
    def _():
        wbf_ref[...] = wt_ref[...].astype(BF16)

    o_ref[...] = lax.dot_general(a_ref[...], wbf_ref[...], (((1,), (1,)), ((), ())),
                                 preferred_element_type=F32).astype(o_ref.dtype)


def _matmul(a, wt3, layer, tile0, n_out, out_dtype, tn, name):
    m, k = a.shape
    tm = MM_TM
    assert m % tm == 0 and n_out % tn == 0 and wt3.shape[2] == k
    return pl.pallas_call(
        _mm_kernel,
        grid=(n_out // tn, m // tm),
        in_specs=[pl.BlockSpec((tm, k), lambda j, i: (i, 0)),
                  pl.BlockSpec((None, tn, k), lambda j, i: (layer, tile0 + j, 0))],
        out_specs=pl.BlockSpec((tm, tn), lambda j, i: (i, j)),
        out_shape=jax.ShapeDtypeStruct((m, n_out), out_dtype),
        scratch_shapes=[pltpu.VMEM((tn, k), BF16)],
        compiler_params=_params("arbitrary", "arbitrary"),
        name=name,
    )(a, wt3)


def _proj_ln_kernel(*refs, n_parts, n_cast, alpha):
    a_refs = refs[:n_parts]
    w_ref, res_ref, g_ref, b_ref, o_ref, obf_ref, wbf_ref = refs[n_parts:]
    s = pl.program_id(0)
    rows = w_ref.shape[0]

    @pl.when(s < n_cast)
    def _():
        r0 = pl.multiple_of(s * rows, rows)
        wbf_ref[pl.ds(r0, rows), :] = w_ref[...].astype(BF16)

    @pl.when(s >= n_cast)
    def _():
        acc = None
        k0 = 0
        for a_ref in a_refs:
            kp = a_ref.shape[1]
            part = jnp.dot(a_ref[...], wbf_ref[k0:k0 + kp, :], preferred_element_type=F32)
            acc = part if acc is None else acc + part
            k0 += kp
        y = alpha * res_ref[...] + acc
        out = _layer_norm(y, g_ref[...], b_ref[...], LN_EPS)
        o_ref[...] = out
        obf_ref[...] = out.astype(BF16)


def _proj_ln(a_parts, w3, layer, res, g, b, alpha, tm, name):
    m, d = res.shape
    k = w3.shape[1]
    rows = CAST_ROWS
    n_cast = k // rows
    assert m % tm == 0 and k % rows == 0 and sum(a.shape[1] for a in a_parts) == k
    tok = lambda s: (jnp.maximum(s - n_cast, 0), 0)
    in_specs = [pl.BlockSpec((tm, a.shape[1]), tok) for a in a_parts]
    in_specs += [pl.BlockSpec((None, rows, d), lambda s: (layer, jnp.minimum(s, n_cast - 1), 0)),
                 pl.BlockSpec((tm, d), tok),
                 pl.BlockSpec((1, d), lambda s: (0, 0)),
                 pl.BlockSpec((1, d), lambda s: (0, 0))]
    return pl.pallas_call(
        functools.partial(_proj_ln_kernel, n_parts=len(a_parts), n_cast=n_cast, alpha=alpha),
        grid=(n_cast + m // tm,),
        in_specs=in_specs,
        out_specs=[pl.BlockSpec((tm, d), tok), pl.BlockSpec((tm, d), tok)],
        out_shape=[jax.ShapeDtypeStruct((m, d), F32), jax.ShapeDtypeStruct((m, d), BF16)],
        scratch_shapes=[pltpu.VMEM((k, d), BF16)],
        compiler_params=_params("arbitrary"),
        name=name,
    )(*a_parts, w3, res, g.reshape(1, d), b.reshape(1, d))


def _sb_kernel(q_ref, k_ref, v_ref, o_ref, *, tq, dh, scale):
    i = pl.program_id(2)
    n_h = q_ref.shape[1] // dh
    heads = range(n_h)
    cols = [slice(hh * dh, (hh + 1) * dh) for hh in heads]
    row = lax.broadcasted_iota(jnp.int32, (tq, tq), 0)
    col = lax.broadcasted_iota(jnp.int32, (tq, tq), 1)
    after = jnp.where(row > col, 1.0, 0.0).astype(BF16)
    causal = col < row

    def tile(j, state, diagonal):
        k0 = pl.multiple_of(j * tq, tq)
        z = [lax.dot_general(q_ref[:, cols[hh]], k_ref[pl.ds(k0, tq), cols[hh]],
                             (((1,), (1,)), ((), ())), preferred_element_type=F32) * scale
             for hh in heads]
        ls = [_log_sigmoid(z[hh]) for hh in heads]
        lk = [ls[hh] - z[hh] for hh in heads]
        if diagonal:
            lk = [jnp.where(causal, lk[hh], 0.0) for hh in heads]
        suffix = [_dot_hilo(lk[hh], after) + state[hh][0] for hh in heads]
        w = [jnp.exp(ls[hh] + suffix[hh]) for hh in heads]
        if diagonal:
            w = [jnp.where(causal, w[hh], 0.0) for hh in heads]
        acc = [state[hh][1] + jnp.dot(w[hh].astype(BF16), v_ref[pl.ds(k0, tq), cols[hh]],
                                      preferred_element_type=F32) for hh in heads]
        carry = [suffix[hh][:, 0:1] + lk[hh][:, 0:1] for hh in heads]
        return tuple((carry[hh], acc[hh]) for hh in heads)

    init = tuple((jnp.zeros((tq, 1), F32), jnp.zeros((tq, dh), F32)) for _ in heads)
    state = tile(i, init, True)

    def live(st):
        top = st[0][0]
        for hh in heads[1:]:
            top = jnp.maximum(top, st[hh][0])
        return (jnp.max(top) >= SB_LOG_UNDERFLOW).astype(jnp.int32)

    def more(loop):
        return jnp.logical_and(loop[0] < i, loop[1] > 0)

    def step(loop):
        st = tile(i - 1 - loop[0], loop[2], False)
        return loop[0] + 1, live(st), st

    state = lax.while_loop(more, step, (jnp.int32(0), live(state), state))[2]
    for hh in heads:
        o_ref[:, cols[hh]] = state[hh][1].astype(o_ref.dtype)


def _sb_attention(u_sb, batch, seq):
    tq = SB_TQ
    nq = seq // tq
    dh = SB_HEAD_DIM
    g = SB_HEADS_PER_STEP
    ng = SB_HEADS // g
    wd = g * dh
    assert seq % tq == 0 and dh % LANES == 0 and SB_HEADS % g == 0
    return pl.pallas_call(
        functools.partial(_sb_kernel, tq=tq, dh=dh, scale=dh ** -0.5),
        grid=(batch, ng, nq),
        in_specs=[pl.BlockSpec((tq, wd), lambda b, hh, i: (b * nq + i, hh)),
                  pl.BlockSpec((seq, wd), lambda b, hh, i: (b, ng + hh)),
                  pl.BlockSpec((seq, wd), lambda b, hh, i: (b, 2 * ng + hh))],
        out_specs=pl.BlockSpec((tq, wd), lambda b, hh, i: (b * nq + i, hh)),
        out_shape=jax.ShapeDtypeStruct((batch * seq, SB_HEADS * dh), BF16),
        compiler_params=_params("arbitrary", "arbitrary", "arbitrary"),
        name="sb_attention",
    )(u_sb, u_sb, u_sb)


def _token_shift(u, prev_ref, mu):
    rows = u.shape[0]
    row1 = lax.broadcasted_iota(jnp.int32, (rows, 1), 0)
    prev = jnp.where(row1 == 0, prev_ref[SUBLANES - 1:SUBLANES, :], pltpu.roll(u, 1, axis=0))
    prev_ref[...] = u[rows - SUBLANES:, :]
    return u + (prev - u) * mu


def _rwkv_kernel(u_ref, ul_ref, mu_ref, mul_ref, w0_ref, w2_ref, a0_ref, a2_ref, g2_ref, kk_ref,
                 ka_ref, rk_ref, gng_ref, gnb_ref, o_ref, state_ref, prev_ref, prevl_ref,
                 *, chunk, width):
    L, C = chunk, width
    nb = u_ref.shape[0]
    n_pairs = C // LANES
    N = LANES // RW_PAIR
    R = nb * L
    c = pl.program_id(1)

    @pl.when(c == 0)
    def _():
        state_ref[...] = jnp.zeros_like(state_ref)
        prev_ref[...] = jnp.zeros_like(prev_ref)
        prevl_ref[...] = jnp.zeros_like(prevl_ref)

    us = jnp.concatenate([_token_shift(u_ref[bb], prev_ref.at[bb], mu_ref[...])
                          for bb in range(nb)], axis=0)
    ul = jnp.concatenate([_token_shift(ul_ref[bb], prevl_ref.at[bb], mul_ref[...])
                          for bb in range(nb)], axis=0)
    r = us[:, 0:C]
    k = us[:, C:2 * C]
    v = us[:, 2 * C:3 * C]
    o1 = _round_up(RW_DECAY_LORA, LANES)
    o2 = o1 + _round_up(RW_AAA_LORA, LANES)
    dw = ul[:, :o1]
    da = ul[:, o1:o2]
    dg = ul[:, o2:]

    w_log = _log_sigmoid(w0_ref[...] + _dot(jnp.tanh(dw), w2_ref[...])) - 0.5
    log_decay = -jnp.exp(w_log)
    a = _sigmoid(a0_ref[...] + _dot(da, a2_ref[...]))
    gate = _dot(_sigmoid(dg), g2_ref[...])

    rowl = lax.broadcasted_iota(jnp.int32, (R, R), 0)
    coll = lax.broadcasted_iota(jnp.int32, (R, R), 1)
    lower01 = jnp.where(coll <= rowl, jnp.where(coll >= (rowl // L) * L, 1.0, 0.0),
                        0.0).astype(BF16)
    lc = _cumsum_dot(lower01, log_decay)
    g_incl = jnp.exp(lc)
    g_excl = jnp.exp(lc - log_decay)
    g_inv = jnp.exp(-lc)
    g_last = [g_incl[(bb + 1) * L - 1:(bb + 1) * L, :] for bb in range(nb)]
    g_last_rows = jnp.concatenate([jnp.broadcast_to(g_last[bb], (L, C)) for bb in range(nb)],
                                  axis=0)

    lane = lax.broadcasted_iota(jnp.int32, (1, LANES), 1)
    first = lane < N

    def head_sum(x):
        out = []
        for p in range(n_pairs):
            xs = x[:, p * LANES:(p + 1) * LANES]
            s1 = jnp.sum(jnp.where(first, xs, 0.0), axis=-1, keepdims=True)
            s2 = jnp.sum(jnp.where(first, 0.0, xs), axis=-1, keepdims=True)
            out.append(jnp.where(first, s1, s2))
        return jnp.concatenate(out, axis=1)

    kkf = k * kk_ref[...]
    kk = kkf * lax.rsqrt(jnp.maximum(head_sum(kkf * kkf), 1e-24))
    kmod = k * (1.0 + (a - 1.0) * ka_ref[...])
    at_all = -(kk * g_excl)
    bt_all = kk * a * g_inv
    kt_all = kmod * g_inv
    rt_all = r * g_incl
    be_all = bt_all * g_last_rows
    ke_all = kt_all * g_last_rows
    bonus = head_sum(r * kmod * rk_ref[...]) * v

    t_idx = lax.broadcasted_iota(jnp.int32, (L, LANES), 0)
    j_idx = lax.broadcasted_iota(jnp.int32, (L, LANES), 1)
    strict = t_idx > j_idx % N
    incl = t_idx >= j_idx % N
    keep1 = jnp.where(j_idx < N, 1.0, 0.0).astype(BF16)
    keep2 = jnp.where(j_idx < N, 0.0, 1.0).astype(BF16)

    def bd(x):
        xb = x.astype(BF16)
        return jnp.concatenate([xb * keep1, xb * keep2], axis=0)

    units = [(bb, p) for bb in range(nb) for p in range(n_pairs)]
    nu = range(len(units))
    rs = [slice(bb * L, (bb + 1) * L) for bb, _ in units]
    sl = [slice(p * LANES, (p + 1) * LANES) for _, p in units]
    lhs = [jnp.concatenate([at_all[rs[u], sl[u]], rt_all[rs[u], sl[u]]], axis=0) for u in nu]
    s0 = [state_ref[bb, p] for bb, p in units]
    g = [_dot_nt(lhs[u], jnp.concatenate([bd(bt_all[rs[u], sl[u]]), bd(kt_all[rs[u], sl[u]])],
                                         axis=0)) for u in nu]
    ss = [_dot_nt(lhs[u], bd(s0[u])) for u in nu]
    pw = [jnp.where(strict, g[u][:L, :LANES], 0.0) for u in nu]
    a_ak = [jnp.where(strict, g[u][:L, LANES:], 0.0) for u in nu]
    a_rb = [jnp.where(incl, g[u][L:, :LANES], 0.0) for u in nu]
    a_rk = [jnp.where(incl, g[u][L:, LANES:], 0.0) for u in nu]
    av = [_dot(jnp.concatenate([a_ak[u], a_rk[u]], axis=0), bd(v[rs[u], sl[u]])) for u in nu]
    nxt = [_dot(pw[u], bd(pw[u])) for u in nu]
    x = [ss[u][:L] + av[u][:L] for u in nu]
    span = 1
    while True:
        x = [x[u] + _dot(pw[u], bd(x[u])) for u in nu]
        if 2 * span >= L:
            break
        pw = nxt
        span *= 2
        if 2 * span < L:
            nxt = [_dot(pw[u], bd(pw[u])) for u in nu]
    y = [ss[u][L:] + av[u][L:] + _dot(a_rb[u], bd(x[u])) for u in nu]
    for u, (bb, p) in enumerate(units):
        full = _dot_tn(jnp.concatenate([x[u], v[rs[u], sl[u]]], axis=0),
                       jnp.concatenate([be_all[rs[u], sl[u]], ke_all[rs[u], sl[u]]], axis=0))
        state_ref[bb, p] = s0[u] * g_last[bb][:, sl[u]] + jnp.where(first, full[:N], full[N:])

    y = jnp.concatenate([jnp.concatenate(y[bb * n_pairs:(bb + 1) * n_pairs], axis=1)
                         for bb in range(nb)], axis=0)
    d = y - head_sum(y) * (1.0 / N)
    var = head_sum(d * d) * (1.0 / N)
    out = (d * lax.rsqrt(var + RW_GN_EPS) * gng_ref[...] + gnb_ref[...] + bonus) * gate
    for bb in range(nb):
        o_ref[bb] = out[bb * L:(bb + 1) * L, :].astype(o_ref.dtype)


def _rwkv(u_rkv, u_lora, batch, seq, mu, mu_lora, w0, w2, a0, a2, g2, k_k, k_a, r_k, gn_g, gn_b):
    L = RW_CHUNK
    nb = RW_BATCH
    width = w0.shape[0]
    n = RW_HEAD_DIM
    nc = seq // L
    wu = u_rkv.shape[1]
    wl = u_lora.shape[1]
    assert seq % L == 0 and L % SUBLANES == 0 and width % LANES == 0 and L == n
    assert wu == 3 * width and batch % nb == 0
    row = lambda x: x.reshape(1, -1)
    const = lambda shape: pl.BlockSpec(shape, lambda b, c: (0, 0))
    out = pl.pallas_call(
        functools.partial(_rwkv_kernel, chunk=L, width=width),
        grid=(batch // nb, nc),
        in_specs=[pl.BlockSpec((nb, L, wu), lambda b, c: (b, c, 0)),
                  pl.BlockSpec((nb, L, wl), lambda b, c: (b, c, 0)),
                  const((1, wu)), const((1, wl)), const((1, width)), const(w2.shape),
                  const((1, width)), const(a2.shape), const(g2.shape), const((1, width)),
                  const((1, width)), const((1, width)), const((1, width)), const((1, width))],
        out_specs=pl.BlockSpec((nb, L, width), lambda b, c: (b, c, 0)),
        out_shape=jax.ShapeDtypeStruct((batch, seq, width), BF16),
        scratch_shapes=[pltpu.VMEM((nb, width // LANES, n, LANES), F32),
                        pltpu.VMEM((nb, SUBLANES, wu), F32), pltpu.VMEM((nb, SUBLANES, wl), F32)],
        compiler_params=_params("arbitrary", "arbitrary"),
        name="rwkv7",
    )(u_rkv.reshape(batch, seq, wu), u_lora.reshape(batch, seq, wl), row(mu), row(mu_lora),
      row(w0), w2, row(a0), a2, g2, row(k_k), row(k_a), row(r_k), row(gn_g), row(gn_b))
    return out.reshape(batch * seq, width)


def _gla_kernel(q_ref, k_ref, v_ref, r_ref, dg_ref, gw_ref, gb_ref, rb_ref, gng_ref, gnb_ref,
                o_ref, state_ref, *, chunk, n_h, scale):
    L = chunk
    tb = q_ref.shape[0]
    dk = q_ref.shape[1] // n_h
    dv = v_ref.shape[1] // n_h
    n_ch = tb // L
    heads = range(n_h)
    chunks = range(n_ch)
    kc = [slice(hh * dk, (hh + 1) * dk) for hh in heads]
    vc = [slice(hh * dv, (hh + 1) * dv) for hh in heads]
    rc = [slice(c * L, (c + 1) * L) for c in chunks]
    t = pl.program_id(2)

    @pl.when(t == 0)
    def _():
        state_ref[...] = jnp.zeros_like(state_ref)

    row = lax.broadcasted_iota(jnp.int32, (tb, tb), 0)
    col = lax.broadcasted_iota(jnp.int32, (tb, tb), 1)
    chunk_start = (row // L) * L
    lower = jnp.where(col <= row, jnp.where(col >= chunk_start, 1.0, 0.0), 0.0)
    lower01 = lower.astype(BF16)
    in_chunk = lower > 0.0

    log_a = _log_sigmoid(_dot_split(dg_ref[...], gw_ref[...]) + gb_ref[...]) / GLA_TAU
    b = _cumsum_dot(lower01, log_a)
    eb = jnp.exp(b)
    enb = jnp.exp(-b)
    decay = [eb[(c + 1) * L - 1:(c + 1) * L, :] for c in chunks]
    eb_last = jnp.concatenate([jnp.broadcast_to(decay[c], (L, n_h * dk)) for c in chunks], axis=0)
    q_dec = (q_ref[...] * scale * eb).astype(BF16)
    k_inv = k_ref[...] * enb
    k_end = (k_inv * eb_last).astype(BF16)
    k_inv = k_inv.astype(BF16)
    v = v_ref[...].astype(BF16)
    att = [jnp.where(in_chunk, _dot_nt(q_dec[:, kc[hh]], k_inv[:, kc[hh]]), 0.0) for hh in heads]
    o = [_dot(att[hh], v[:, vc[hh]]) for hh in heads]
    kv = [[_dot_tn(v[rc[c], vc[hh]], k_end[rc[c], kc[hh]]) for c in chunks] for hh in heads]
    st = [state_ref[hh] for hh in heads]
    inter = [[] for _ in heads]
    for c in chunks:
        for hh in heads:
            inter[hh].append(_dot_nt(q_dec[rc[c], kc[hh]], st[hh]))
            st[hh] = st[hh] * decay[c][:, kc[hh]] + kv[hh][c]
    for hh in heads:
        state_ref[hh] = st[hh]
        oh = o[hh] + jnp.concatenate(inter[hh], axis=0)
        oh = _layer_norm(oh, gng_ref[:, vc[hh]], gnb_ref[:, vc[hh]], LN_EPS)
        x = r_ref[:, vc[hh]] + rb_ref[:, vc[hh]]
        o_ref[:, vc[hh]] = (oh * (x * _sigmoid(x))).astype(o_ref.dtype)


def _gla(u_qkv, u_r, dg, batch, seq, gate_w2, gate_b, r_b, gn_g, gn_b):
    tb = GLA_TB
    g = GLA_HEADS_PER_STEP
    key = gate_b.shape[0]
    val = r_b.shape[0]
    dk, dv = key // GLA_HEADS, val // GLA_HEADS
    gk, gv = g * dk, g * dv
    ng = GLA_HEADS // g
    nt = seq // tb
    rank = gate_w2.shape[0]
    assert seq % tb == 0 and tb % GLA_CHUNK == 0 and GLA_HEADS % g == 0
    assert (2 * key) % gv == 0 and dg.shape[1] == rank
    row = lambda x: x.reshape(1, -1)
    return pl.pallas_call(
        functools.partial(_gla_kernel, chunk=GLA_CHUNK, n_h=g, scale=dk ** -0.5),
        grid=(batch, ng, nt),
        in_specs=[pl.BlockSpec((tb, gk), lambda b, hh, t: (b * nt + t, hh)),
                  pl.BlockSpec((tb, gk), lambda b, hh, t: (b * nt + t, ng + hh)),
                  pl.BlockSpec((tb, gv), lambda b, hh, t: (b * nt + t, 2 * key // gv + hh)),
                  pl.BlockSpec((tb, gv), lambda b, hh, t: (b * nt + t, hh)),
                  pl.BlockSpec((tb, rank), lambda b, hh, t: (b * nt + t, 0)),
                  pl.BlockSpec((rank, gk), lambda b, hh, t: (0, hh)),
                  pl.BlockSpec((1, gk), lambda b, hh, t: (0, hh)),
                  pl.BlockSpec((1, gv), lambda b, hh, t: (0, hh)),
                  pl.BlockSpec((1, gv), lambda b, hh, t: (0, hh)),
                  pl.BlockSpec((1, gv), lambda b, hh, t: (0, hh))],
        out_specs=pl.BlockSpec((tb, gv), lambda b, hh, t: (b * nt + t, hh)),
        out_shape=jax.ShapeDtypeStruct((batch * seq, val), BF16),
        scratch_shapes=[pltpu.VMEM((g, dv, dk), F32)],
        compiler_params=_params("arbitrary", "arbitrary", "arbitrary"),
        name="gla",
    )(u_qkv, u_qkv, u_qkv, u_r, dg, gate_w2, row(gate_b), row(r_b), row(gn_g), row(gn_b))


def _ffn_up_kernel(a_ref, wg_ref, wu_ref, cw_ref, cb_ref, o_ref, carry_ref, wgbf_ref, wubf_ref,
                   *, seq):
    i = pl.program_id(1)
    tm = a_ref.shape[0]

    @pl.when(i == 0)
    def _():
        wgbf_ref[...] = wg_ref[...].astype(BF16)
        wubf_ref[...] = wu_ref[...].astype(BF16)

    a = a_ref[...]
    gate = jnp.dot(a, wgbf_ref[...], preferred_element_type=F32)
    up = jnp.dot(a, wubf_ref[...], preferred_element_type=F32)
    first = (i * tm) % seq == 0
    tail = carry_ref[...]
    carry_ref[...] = gate[tm - SUBLANES:, :]
    p1 = jnp.where(first, 0.0, tail[SUBLANES - 1:SUBLANES, :])
    p2 = jnp.where(first, 0.0, tail[SUBLANES - 2:SUBLANES - 1, :])
    row = lax.broadcasted_iota(jnp.int32, (tm, 1), 0)
    g1 = jnp.where(row == 0, p1, pltpu.roll(gate, 1, axis=0))
    g2 = jnp.where(row == 0, p2, jnp.where(row == 1, p1, pltpu.roll(gate, 2, axis=0)))
    conv = cw_ref[0:1, :] * g2 + cw_ref[1:2, :] * g1 + cw_ref[2:3, :] * gate + cb_ref[...]
    act = 0.5 * conv * (1.0 + lax.erf(conv * math.sqrt(0.5)))
    o_ref[...] = (act * up).astype(o_ref.dtype)


def _ffn_up(a, w_up3, layer, conv_w3, conv_b2, seq):
    m, k = a.shape
    dff = conv_b2.shape[1]
    tm, tn = FFN_TM, FFN_TN
    nj = dff // tn
    assert m % tm == 0 and dff % tn == 0 and seq % tm == 0 and conv_w3.shape[1] == CONV_W
    assert w_up3.shape[2] == 2 * dff
    return pl.pallas_call(
        functools.partial(_ffn_up_kernel, seq=seq),
        grid=(nj, m // tm),
        in_specs=[pl.BlockSpec((tm, k), lambda j, i: (i, 0)),
                  pl.BlockSpec((None, k, tn), lambda j, i: (layer, 0, j)),
                  pl.BlockSpec((None, k, tn), lambda j, i: (layer, 0, nj + j)),
                  pl.BlockSpec((None, CONV_W, tn), lambda j, i: (layer, 0, j)),
                  pl.BlockSpec((None, 1, tn), lambda j, i: (layer, 0, j))],
        out_specs=pl.BlockSpec((tm, tn), lambda j, i: (i, j)),
        out_shape=jax.ShapeDtypeStruct((m, dff), BF16),
        scratch_shapes=[pltpu.VMEM((SUBLANES, tn), F32), pltpu.VMEM((k, tn), BF16),
                        pltpu.VMEM((k, tn), BF16)],
        compiler_params=_params("arbitrary", "arbitrary"),
        name="ffn_up",
    )(a, w_up3, w_up3, conv_w3, conv_b2.reshape(conv_b2.shape[0], 1, dff))


def _pad_cols(w, n):
    return jnp.pad(w, ((0, 0), (0, n - w.shape[1])))


def _pad_rows(w, n):
    return jnp.pad(w, ((0, n - w.shape[0]), (0, 0)))


def _pad_vec(w, n):
    return jnp.pad(w, (0, n - w.shape[0]))


def kernel(x, even_w_in, even_shift_mu, rw_w0, rw_w2, rw_a0, rw_a2, rw_g2, rw_k_k, rw_k_a, rw_r_k, rw_gn_g, rw_gn_b, even_w_out, odd_w_in, gla_gate_w2, gla_gate_b, gla_r_b, gla_gn_g, gla_gn_b, odd_w_out, ln_mix_g, ln_mix_b, ln_ffn_g, ln_ffn_b, ffn_w_up, ffn_conv_w, ffn_conv_b, ffn_w_down):
    batch, seq, d = x.shape
    depth = ln_mix_g.shape[0]
    alpha = (2 * depth) ** 0.25
    m = batch * seq
    h = x.reshape(m, d)
    h_bf = h.astype(BF16)

    sb_w = SB_HEADS * SB_HEAD_DIM
    rw_w = rw_w0.shape[1]
    lw, la, lg = (_round_up(n, LANES) for n in (RW_DECAY_LORA, RW_AAA_LORA, RW_GATE_LORA))
    tn = MM_TN
    even_wt = jnp.swapaxes(even_w_in, 1, 2)
    odd_wt = jnp.swapaxes(odd_w_in, 1, 2)

    for layer in range(depth):
        i = layer // 2
        if layer % 2 == 0:
            c0 = 3 * sb_w
            c1 = c0 + 3 * rw_w
            c2 = c1 + RW_DECAY_LORA
            c3 = c2 + RW_AAA_LORA
            assert c0 % tn == 0 and c1 % tn == 0
            wt_lora = jnp.concatenate([_pad_rows(even_wt[i, c1:c2], lw),
                                       _pad_rows(even_wt[i, c2:c3], la),
                                       _pad_rows(even_wt[i, c3:], lg)], axis=0)
            n_lora = _round_up(wt_lora.shape[0], LORA_TN)
            wt_lora = _pad_rows(wt_lora, n_lora)[None]
            mu = even_shift_mu[i]
            m0 = 3 * rw_w
            m1 = m0 + RW_DECAY_LORA
            m2 = m1 + RW_AAA_LORA
            mu_lora = _pad_vec(jnp.concatenate([_pad_vec(mu[m0:m1], lw), _pad_vec(mu[m1:m2], la),
                                                _pad_vec(mu[m2:], lg)]), n_lora)
            u_sb = _matmul(h_bf, even_wt, i, 0, c0, BF16, tn, "even_in_sb")
            u_rkv = _matmul(h_bf, even_wt, i, c0 // tn, c1 - c0, F32, tn, "even_in_rkv")
            u_lora = _matmul(h_bf, wt_lora, 0, 0, n_lora, F32, LORA_TN, "even_in_lora")
            o_sb = _sb_attention(u_sb, batch, seq)
            o_rw = _rwkv(u_rkv, u_lora, batch, seq, mu[:m0], mu_lora, rw_w0[i],
                         _pad_rows(rw_w2[i], lw).astype(BF16), rw_a0[i],
                         _pad_rows(rw_a2[i], la).astype(BF16),
                         _pad_rows(rw_g2[i], n_lora - lw - la).astype(BF16),
                         rw_k_k[i], rw_k_a[i], rw_r_k[i].reshape(-1), rw_gn_g[i], rw_gn_b[i])
            h, h_bf = _proj_ln([o_sb, o_rw], even_w_out, i, h, ln_mix_g[layer], ln_mix_b[layer],
                               alpha, LN_TM, "even_out_ln")
        else:
            key = gla_gate_b.shape[1]
            val = gla_r_b.shape[1]
            c0 = 2 * key + val
            c1 = c0 + GLA_GATE_RANK
            rank_p = _round_up(GLA_GATE_RANK, LANES)
            assert c0 % tn == 0 and val % tn == 0
            wt_dg = _pad_rows(odd_wt[i, c0:c1], rank_p)[None]
            u_qkv = _matmul(h_bf, odd_wt, i, 0, c0, F32, tn, "odd_in_qkv")
            u_r = _matmul(h_bf, odd_wt[i, c1:][None], 0, 0, val, F32, tn, "odd_in_r")
            dg = _matmul(h_bf, wt_dg, 0, 0, rank_p, F32, rank_p, "odd_in_gate")
            o = _gla(u_qkv, u_r, dg, batch, seq, _pad_rows(gla_gate_w2[i], rank_p),
                     gla_gate_b[i], gla_r_b[i], gla_gn_g[i], gla_gn_b[i])
            h, h_bf = _proj_ln([o], odd_w_out, i, h, ln_mix_g[layer], ln_mix_b[layer], alpha,
                               LN_TM, "odd_out_ln")
        mid = _ffn_up(h_bf, ffn_w_up, layer, ffn_conv_w, ffn_conv_b, seq)
        h, h_bf = _proj_ln([mid], ffn_w_down, layer, h, ln_ffn_g[layer], ln_ffn_b[layer], alpha,
                           DOWN_TM, "ffn_down_ln")
    return h.reshape(batch, seq, d)
```

```python
import functools
import math

import jax
import jax.numpy as jnp
from jax import lax
from jax.experimental import pallas as pl
from jax.experimental.pallas import tpu as pltpu

F32 = jnp.float32
BF16 = jnp.bfloat16

SB_HEADS = 8
SB_HEAD_DIM = 128
RW_HEAD_DIM = 64
RW_DECAY_LORA = 64
RW_AAA_LORA = 64
RW_GATE_LORA = 160
RW_GN_EPS = 64e-5
GLA_HEADS = 4
GLA_GATE_RANK = 16
GLA_TAU = 16.0
CONV_W = 3
LN_EPS = 1e-5

LANES = 128
SUBLANES = 8
VMEM_LIMIT_BYTES = 56 * 1024 * 1024
MM_TM = 1024
MM_TN = 1024
LORA_TN = 512
LN_TM = 512
FFN_TM = 1024
FFN_TN = 512
DOWN_TM = 256
SB_TQ = 256
SB_HEADS_PER_STEP = 4
SB_LOG_UNDERFLOW = -105.0
RW_CHUNK = 64
RW_PAIR = LANES // RW_HEAD_DIM
RW_BATCH = 2
GLA_CHUNK = 64
GLA_TB = 256
GLA_HEADS_PER_STEP = 4


def _round_up(n, m):
    return (n + m - 1) // m * m


def _params(*sem):
    return pltpu.CompilerParams(dimension_semantics=sem, vmem_limit_bytes=VMEM_LIMIT_BYTES)


def _log_sigmoid(x):
    return jnp.minimum(x, 0.0) - jnp.log(1.0 + jnp.exp(-jnp.abs(x)))


def _sigmoid(x):
    return 1.0 / (1.0 + jnp.exp(-x))


def _dot(a, b):
    return jnp.dot(a.astype(BF16), b.astype(BF16), preferred_element_type=F32)


def _dot_nt(a, b):
    return lax.dot_general(a.astype(BF16), b.astype(BF16), (((1,), (1,)), ((), ())),
                           preferred_element_type=F32)


def _dot_tn(a, b):
    return lax.dot_general(a.astype(BF16), b.astype(BF16), (((0,), (0,)), ((), ())),
                           preferred_element_type=F32)


def _split2(x):
    hi = x.astype(BF16)
    return hi, (x - hi.astype(F32)).astype(BF16)


def _split3(x):
    hi = x.astype(BF16)
    r = x - hi.astype(F32)
    mid = r.astype(BF16)
    return hi, mid, (r - mid.astype(F32)).astype(BF16)


def _dot_hilo(x, w01):
    hi, lo = _split2(x)
    return (jnp.dot(hi, w01, preferred_element_type=F32)
            + jnp.dot(lo, w01, preferred_element_type=F32))


def _cumsum_dot(w01, x):
    return sum(jnp.dot(w01, p, preferred_element_type=F32) for p in _split3(x))


def _dot_split(x, w):
    xh, xl = _split2(x)
    wh, wl = _split2(w)
    return (jnp.dot(xh, wh, preferred_element_type=F32) + jnp.dot(xh, wl, preferred_element_type=F32)
            + jnp.dot(xl, wh, preferred_element_type=F32))


def _layer_norm(y, g, b, eps):
    mu = jnp.mean(y, axis=-1, keepdims=True)
    d = y - mu
    var = jnp.mean(d * d, axis=-1, keepdims=True)
    return d * lax.rsqrt(var + eps) * g + b


def _mm_kernel(a_ref, wt_ref, o_ref, wbf_ref):
    @pl.when(pl.program_id(1) == 0)
    def _():
        wbf_ref[...] = wt_ref[...].astype(BF16)

    o_ref[...] = lax.dot_general(a_ref[...], wbf_ref[...], (((1,), (1,)), ((), ())),
                                 preferred_element_type=F32).astype(o_ref.dtype)


def _matmul(a, wt3, layer, tile0, n_out, out_dtype, tn, name):
    m, k = a.shape
    tm = MM_TM
    assert m % tm == 0 and n_out % tn == 0 and wt3.shape[2] == k
    return pl.pallas_call(
        _mm_kernel,
        grid=(n_out // tn, m // tm),
        in_specs=[pl.BlockSpec((tm, k), lambda j, i: (i, 0)),
                  pl.BlockSpec((None, tn, k), lambda j, i: (layer, tile0 + j, 0))],
        out_specs=pl.BlockSpec((tm, tn), lambda j, i: (i, j)),
        out_shape=jax.ShapeDtypeStruct((m, n_out), out_dtype),
        scratch_shapes=[pltpu.VMEM((tn, k), BF16)],
        compiler_params=_params("arbitrary", "arbitrary"),
        name=name,
    )(a, wt3)


def _mm_cast_kernel(a_ref, wt_ref, side_ref, o_ref, sidebf_ref, wbf_ref):
    _mm_kernel(a_ref, wt_ref, o_ref, wbf_ref)
    sidebf_ref[...] = side_ref[...].astype(BF16)


def _matmul_and_cast(a, wt3, n_out, out_dtype, tn, side3, side_layer, name):
    m, k = a.shape
    tm = MM_TM
    steps = (n_out // tn) * (m // tm)
    rows, d = side3.shape[1], side3.shape[2]
    slab = rows // steps
    ni = m // tm
    assert m % tm == 0 and n_out % tn == 0 and wt3.shape[2] == k
    assert rows % steps == 0 and slab % (2 * SUBLANES) == 0
    return pl.pallas_call(
        _mm_cast_kernel,
        grid=(n_out // tn, ni),
        in_specs=[pl.BlockSpec((tm, k), lambda j, i: (i, 0)),
                  pl.BlockSpec((None, tn, k), lambda j, i: (0, j, 0)),
                  pl.BlockSpec((None, slab, d), lambda j, i: (side_layer, j * ni + i, 0))],
        out_specs=[pl.BlockSpec((tm, tn), lambda j, i: (i, j)),
                   pl.BlockSpec((slab, d), lambda j, i: (j * ni + i, 0))],
        out_shape=[jax.ShapeDtypeStruct((m, n_out), out_dtype),
                   jax.ShapeDtypeStruct((rows, d), BF16)],
        scratch_shapes=[pltpu.VMEM((tn, k), BF16)],
        compiler_params=_params("arbitrary", "arbitrary"),
        name=name,
    )(a, wt3, side3)


def _proj_ln_kernel(*refs, n_parts, alpha):
    a_refs = refs[:n_parts]
    w_ref, res_ref, g_ref, b_ref, o_ref, obf_ref = refs[n_parts:]
    acc = None
    k0 = 0
    for a_ref in a_refs:
        kp = a_ref.shape[1]
        part = jnp.dot(a_ref[...], w_ref[k0:k0 + kp, :], preferred_element_type=F32)
        acc = part if acc is None else acc + part
        k0 += kp
    y = alpha * res_ref[...] + acc
    out = _layer_norm(y, g_ref[...], b_ref[...], LN_EPS)
    o_ref[...] = out
    obf_ref[...] = out.astype(BF16)


def _proj_ln(a_parts, w, res, g, b, alpha, tm, name):
    m, d = res.shape
    k = w.shape[0]
    assert m % tm == 0 and sum(a.shape[1] for a in a_parts) == k and w.dtype == BF16
    tok = lambda s: (s, 0)
    fixed = lambda s: (0, 0)
    in_specs = [pl.BlockSpec((tm, a.shape[1]), tok) for a in a_parts]
    in_specs += [pl.BlockSpec((k, d), fixed, pipeline_mode=pl.Buffered(1)),
                 pl.BlockSpec((tm, d), tok), pl.BlockSpec((1, d), fixed),
                 pl.BlockSpec((1, d), fixed)]
    return pl.pallas_call(
        functools.partial(_proj_ln_kernel, n_parts=len(a_parts), alpha=alpha),
        grid=(m // tm,),
        in_specs=in_specs,
        out_specs=[pl.BlockSpec((tm, d), tok), pl.BlockSpec((tm, d), tok)],
        out_shape=[jax.ShapeDtypeStruct((m, d), F32), jax.ShapeDtypeStruct((m, d), BF16)],
        compiler_params=_params("arbitrary"),
        name=name,
    )(*a_parts, w, res, g.reshape(1, d), b.reshape(1, d))


def _sb_kernel(q_ref, k_ref, v_ref, o_ref, *, tq, dh, scale):
    i = pl.program_id(2)
    n_h = q_ref.shape[1] // dh
    heads = range(n_h)
    cols = [slice(hh * dh, (hh + 1) * dh) for hh in heads]
    row = lax.broadcasted_iota(jnp.int32, (tq, tq), 0)
    col = lax.broadcasted_iota(jnp.int32, (tq, tq), 1)
    after = jnp.where(row > col, 1.0, 0.0).astype(BF16)
    causal = col < row

    def tile(j, state, diagonal):
        k0 = pl.multiple_of(j * tq, tq)
        z = [lax.dot_general(q_ref[:, cols[hh]], k_ref[pl.ds(k0, tq), cols[hh]],
                             (((1,), (1,)), ((), ())), preferred_element_type=F32) * scale
             for hh in heads]
        ls = [_log_sigmoid(z[hh]) for hh in heads]
        lk = [ls[hh] - z[hh] for hh in heads]
        if diagonal:
            lk = [jnp.where(causal, lk[hh], 0.0) for hh in heads]
        suffix = [_dot_hilo(lk[hh], after) + state[hh][0] for hh in heads]
        w = [jnp.exp(ls[hh] + suffix[hh]) for hh in heads]
        if diagonal:
            w = [jnp.where(causal, w[hh], 0.0) for hh in heads]
        acc = [state[hh][1] + jnp.dot(w[hh].astype(BF16), v_ref[pl.ds(k0, tq), cols[hh]],
                                      preferred_element_type=F32) for hh in heads]
        carry = [suffix[hh][:, 0:1] + lk[hh][:, 0:1] for hh in heads]
        return tuple((carry[hh], acc[hh]) for hh in heads)

    init = tuple((jnp.zeros((tq, 1), F32), jnp.zeros((tq, dh), F32)) for _ in heads)
    state = tile(i, init, True)

    def live(st):
        top = st[0][0]
        for hh in heads[1:]:
            top = jnp.maximum(top, st[hh][0])
        return (jnp.max(top) >= SB_LOG_UNDERFLOW).astype(jnp.int32)

    def more(loop):
        return jnp.logical_and(loop[0] < i, loop[1] > 0)

    def step(loop):
        st = tile(i - 1 - loop[0], loop[2], False)
        return loop[0] + 1, live(st), st

    state = lax.while_loop(more, step, (jnp.int32(0), live(state), state))[2]
    for hh in heads:
        o_ref[:, cols[hh]] = state[hh][1].astype(o_ref.dtype)


def _sb_attention(u_sb, batch, seq):
    tq = SB_TQ
    nq = seq // tq
    dh = SB_HEAD_DIM
    g = SB_HEADS_PER_STEP
    ng = SB_HEADS // g
    wd = g * dh
    assert seq % tq == 0 and dh % LANES == 0 and SB_HEADS % g == 0
    return pl.pallas_call(
        functools.partial(_sb_kernel, tq=tq, dh=dh, scale=dh ** -0.5),
        grid=(batch, ng, nq),
        in_specs=[pl.BlockSpec((tq, wd), lambda b, hh, i: (b * nq + i, hh)),
                  pl.BlockSpec((seq, wd), lambda b, hh, i: (b, ng + hh)),
                  pl.BlockSpec((seq, wd), lambda b, hh, i: (b, 2 * ng + hh))],
        out_specs=pl.BlockSpec((tq, wd), lambda b, hh, i: (b * nq + i, hh)),
        out_shape=jax.ShapeDtypeStruct((batch * seq, SB_HEADS * dh), BF16),
        compiler_params=_params("arbitrary", "arbitrary", "arbitrary"),
        name="sb_attention",
    )(u_sb, u_sb, u_sb)


def _token_shift(u, prev_ref, mu):
    rows = u.shape[0]
    row1 = lax.broadcasted_iota(jnp.int32, (rows, 1), 0)
    prev = jnp.where(row1 == 0, prev_ref[SUBLANES - 1:SUBLANES, :], pltpu.roll(u, 1, axis=0))
    prev_ref[...] = u[rows - SUBLANES:, :]
    return u + (prev - u) * mu


def _rwkv_kernel(u_ref, ul_ref, mu_ref, mul_ref, w0_ref, w2_ref, a0_ref, a2_ref, g2_ref, kk_ref,
                 ka_ref, rk_ref, gng_ref, gnb_ref, o_ref, state_ref, prev_ref, prevl_ref,
                 *, chunk, width):
    L, C = chunk, width
    nb = u_ref.shape[0]
    n_pairs = C // LANES
    N = LANES // RW_PAIR
    R = nb * L
    c = pl.program_id(1)

    @pl.when(c == 0)
    def _():
        state_ref[...] = jnp.zeros_like(state_ref)
        prev_ref[...] = jnp.zeros_like(prev_ref)
        prevl_ref[...] = jnp.zeros_like(prevl_ref)

    us = jnp.concatenate([_token_shift(u_ref[bb], prev_ref.at[bb], mu_ref[...])
                          for bb in range(nb)], axis=0)
    ul = jnp.concatenate([_token_shift(ul_ref[bb], prevl_ref.at[bb], mul_ref[...])
                          for bb in range(nb)], axis=0)
    r = us[:, 0:C]
    k = us[:, C:2 * C]
    v = us[:, 2 * C:3 * C]
    o1 = _round_up(RW_DECAY_LORA, LANES)
    o2 = o1 + _round_up(RW_AAA_LORA, LANES)
    dw = ul[:, :o1]
    da = ul[:, o1:o2]
    dg = ul[:, o2:]

    w_log = _log_sigmoid(w0_ref[...] + _dot(jnp.tanh(dw), w2_ref[...])) - 0.5
    log_decay = -jnp.exp(w_log)
    a = _sigmoid(a0_ref[...] + _dot(da, a2_ref[...]))
    gate = _dot(_sigmoid(dg), g2_ref[...])

    rowl = lax.broadcasted_iota(jnp.int32, (R, R), 0)
    coll = lax.broadcasted_iota(jnp.int32, (R, R), 1)
    lower01 = jnp.where(coll <= rowl, jnp.where(coll >= (rowl // L) * L, 1.0, 0.0),
                        0.0).astype(BF16)
    lc = _cumsum_dot(lower01, log_decay)
    g_incl = jnp.exp(lc)
    g_excl = jnp.exp(lc - log_decay)
    g_inv = jnp.exp(-lc)
    g_last = [g_incl[(bb + 1) * L - 1:(bb + 1) * L, :] for bb in range(nb)]
    g_last_rows = jnp.concatenate([jnp.broadcast_to(g_last[bb], (L, C)) for bb in range(nb)],
                                  axis=0)

    lane = lax.broadcasted_iota(jnp.int32, (1, LANES), 1)
    first = lane < N

    def head_sum(x):
        out = []
        for p in range(n_pairs):
            xs = x[:, p * LANES:(p + 1) * LANES]
            s1 = jnp.sum(jnp.where(first, xs, 0.0), axis=-1, keepdims=True)
            s2 = jnp.sum(jnp.where(first, 0.0, xs), axis=-1, keepdims=True)
            out.append(jnp.where(first, s1, s2))
        return jnp.concatenate(out, axis=1)

    kkf = k * kk_ref[...]
    kk = kkf * lax.rsqrt(jnp.maximum(head_sum(kkf * kkf), 1e-24))
    kmod = k * (1.0 + (a - 1.0) * ka_ref[...])
    at_all = -(kk * g_excl)
    bt_all = kk * a * g_inv
    kt_all = kmod * g_inv
    rt_all = r * g_incl
    be_all = bt_all * g_last_rows
    ke_all = kt_all * g_last_rows
    bonus = head_sum(r * kmod * rk_ref[...]) * v

    t_idx = lax.broadcasted_iota(jnp.int32, (L, LANES), 0)
    j_idx = lax.broadcasted_iota(jnp.int32, (L, LANES), 1)
    strict = t_idx > j_idx % N
    incl = t_idx >= j_idx % N
    keep1 = jnp.where(j_idx < N, 1.0, 0.0).astype(BF16)
    keep2 = jnp.where(j_idx < N, 0.0, 1.0).astype(BF16)

    def bd(x):
        xb = x.astype(BF16)
        return jnp.concatenate([xb * keep1, xb * keep2], axis=0)

    units = [(bb, p) for bb in range(nb) for p in range(n_pairs)]
    nu = range(len(units))
    rs = [slice(bb * L, (bb + 1) * L) for bb, _ in units]
    sl = [slice(p * LANES, (p + 1) * LANES) for _, p in units]
    lhs = [jnp.concatenate([at_all[rs[u], sl[u]], rt_all[rs[u], sl[u]]], axis=0) for u in nu]
    s0 = [state_ref[bb, p] for bb, p in units]
    g = [_dot_nt(lhs[u], jnp.concatenate([bd(bt_all[rs[u], sl[u]]), bd(kt_all[rs[u], sl[u]])],
                                         axis=0)) for u in nu]
    ss = [_dot_nt(lhs[u], bd(s0[u])) for u in nu]
    pw = [jnp.where(strict, g[u][:L, :LANES], 0.0) for u in nu]
    a_ak = [jnp.where(strict, g[u][:L, LANES:], 0.0) for u in nu]
    a_rb = [jnp.where(incl, g[u][L:, :LANES], 0.0) for u in nu]
    a_rk = [jnp.where(incl, g[u][L:, LANES:], 0.0) for u in nu]
    av = [_dot(jnp.concatenate([a_ak[u], a_rk[u]], axis=0), bd(v[rs[u], sl[u]])) for u in nu]
    nxt = [_dot(pw[u], bd(pw[u])) for u in nu]
    x = [ss[u][:L] + av[u][:L] for u in nu]
    span = 1
    while True:
        x = [x[u] + _dot(pw[u], bd(x[u])) for u in nu]
        if 2 * span >= L:
            break
        pw = nxt
        span *= 2
        if 2 * span < L:
            nxt = [_dot(pw[u], bd(pw[u])) for u in nu]
    y = [ss[u][L:] + av[u][L:] + _dot(a_rb[u], bd(x[u])) for u in nu]
    for u, (bb, p) in enumerate(units):
        full = _dot_tn(jnp.concatenate([x[u], v[rs[u], sl[u]]], axis=0),
                       jnp.concatenate([be_all[rs[u], sl[u]], ke_all[rs[u], sl[u]]], axis=0))
        state_ref[bb, p] = s0[u] * g_last[bb][:, sl[u]] + jnp.where(first, full[:N], full[N:])

    y = jnp.concatenate([jnp.concatenate(y[bb * n_pairs:(bb + 1) * n_pairs], axis=1)
                         for bb in range(nb)], axis=0)
    d = y - head_sum(y) * (1.0 / N)
    var = head_sum(d * d) * (1.0 / N)
    out = (d * lax.rsqrt(var + RW_GN_EPS) * gng_ref[...] + gnb_ref[...] + bonus) * gate
    for bb in range(nb):
        o_ref[bb] = out[bb * L:(bb + 1) * L, :].astype(o_ref.dtype)


def _rwkv(u_rkv, u_lora, batch, seq, mu, mu_lora, w0, w2, a0, a2, g2, k_k, k_a, r_k, gn_g, gn_b):
    L = RW_CHUNK
    nb = RW_BATCH
    width = w0.shape[0]
    n = RW_HEAD_DIM
    nc = seq // L
    wu = u_rkv.shape[1]
    wl = u_lora.shape[1]
    assert seq % L == 0 and L % SUBLANES == 0 and width % LANES == 0 and L == n
    assert wu == 3 * width and batch % nb == 0
    row = lambda x: x.reshape(1, -1)
    const = lambda shape: pl.BlockSpec(shape, lambda b, c: (0, 0))
    out = pl.pallas_call(
        functools.partial(_rwkv_kernel, chunk=L, width=width),
        grid=(batch // nb, nc),
        in_specs=[pl.BlockSpec((nb, L, wu), lambda b, c: (b, c, 0)),
                  pl.BlockSpec((nb, L, wl), lambda b, c: (b, c, 0)),
                  const((1, wu)), const((1, wl)), const((1, width)), const(w2.shape),
                  const((1, width)), const(a2.shape), const(g2.shape), const((1, width)),
                  const((1, width)), const((1, width)), const((1, width)), const((1, width))],
        out_specs=pl.BlockSpec((nb, L, width), lambda b, c: (b, c, 0)),
        out_shape=jax.ShapeDtypeStruct((batch, seq, width), BF16),
        scratch_shapes=[pltpu.VMEM((nb, width // LANES, n, LANES), F32),
                        pltpu.VMEM((nb, SUBLANES, wu), F32), pltpu.VMEM((nb, SUBLANES, wl), F32)],
        compiler_params=_params("arbitrary", "arbitrary"),
        name="rwkv7",
    )(u_rkv.reshape(batch, seq, wu), u_lora.reshape(batch, seq, wl), row(mu), row(mu_lora),
      row(w0), w2, row(a0), a2, g2, row(k_k), row(k_a), row(r_k), row(gn_g), row(gn_b))
    return out.reshape(batch * seq, width)


def _gla_kernel(q_ref, k_ref, v_ref, r_ref, dg_ref, gw_ref, gb_ref, rb_ref, gng_ref, gnb_ref,
                o_ref, state_ref, *, chunk, n_h, scale):
    L = chunk
    tb = q_ref.shape[0]
    dk = q_ref.shape[1] // n_h
    dv = v_ref.shape[1] // n_h
    n_ch = tb // L
    heads = range(n_h)
    chunks = range(n_ch)
    kc = [slice(hh * dk, (hh + 1) * dk) for hh in heads]
    vc = [slice(hh * dv, (hh + 1) * dv) for hh in heads]
    rc = [slice(c * L, (c + 1) * L) for c in chunks]
    t = pl.program_id(2)

    @pl.when(t == 0)
    def _():
        state_ref[...] = jnp.zeros_like(state_ref)

    row = lax.broadcasted_iota(jnp.int32, (tb, tb), 0)
    col = lax.broadcasted_iota(jnp.int32, (tb, tb), 1)
    chunk_start = (row // L) * L
    lower = jnp.where(col <= row, jnp.where(col >= chunk_start, 1.0, 0.0), 0.0)
    lower01 = lower.astype(BF16)
    in_chunk = lower > 0.0

    log_a = _log_sigmoid(_dot_split(dg_ref[...], gw_ref[...]) + gb_ref[...]) / GLA_TAU
    b = _cumsum_dot(lower01, log_a)
    eb = jnp.exp(b)
    enb = jnp.exp(-b)
    decay = [eb[(c + 1) * L - 1:(c + 1) * L, :] for c in chunks]
    eb_last = jnp.concatenate([jnp.broadcast_to(decay[c], (L, n_h * dk)) for c in chunks], axis=0)
    q_dec = (q_ref[...] * scale * eb).astype(BF16)
    k_inv = k_ref[...] * enb
    k_end = (k_inv * eb_last).astype(BF16)
    k_inv = k_inv.astype(BF16)
    v = v_ref[...].astype(BF16)
    att = [jnp.where(in_chunk, _dot_nt(q_dec[:, kc[hh]], k_inv[:, kc[hh]]), 0.0) for hh in heads]
    o = [_dot(att[hh], v[:, vc[hh]]) for hh in heads]
    kv = [[_dot_tn(v[rc[c], vc[hh]], k_end[rc[c], kc[hh]]) for c in chunks] for hh in heads]
    st = [state_ref[hh] for hh in heads]
    inter = [[] for _ in heads]
    for c in chunks:
        for hh in heads:
            inter[hh].append(_dot_nt(q_dec[rc[c], kc[hh]], st[hh]))
            st[hh] = st[hh] * decay[c][:, kc[hh]] + kv[hh][c]
    for hh in heads:
        state_ref[hh] = st[hh]
        oh = o[hh] + jnp.concatenate(inter[hh], axis=0)
        oh = _layer_norm(oh, gng_ref[:, vc[hh]], gnb_ref[:, vc[hh]], LN_EPS)
        x = r_ref[:, vc[hh]] + rb_ref[:, vc[hh]]
        o_ref[:, vc[hh]] = (oh * (x * _sigmoid(x))).astype(o_ref.dtype)


def _gla(u_qkv, u_r, dg, batch, seq, gate_w2, gate_b, r_b, gn_g, gn_b):
    tb = GLA_TB
    g = GLA_HEADS_PER_STEP
    key = gate_b.shape[0]
    val = r_b.shape[0]
    dk, dv = key // GLA_HEADS, val // GLA_HEADS
    gk, gv = g * dk, g * dv
    ng = GLA_HEADS // g
    nt = seq // tb
    rank = gate_w2.shape[0]
    assert seq % tb == 0 and tb % GLA_CHUNK == 0 and GLA_HEADS % g == 0
    assert (2 * key) % gv == 0 and dg.shape[1] == rank
    row = lambda x: x.reshape(1, -1)
    return pl.pallas_call(
        functools.partial(_gla_kernel, chunk=GLA_CHUNK, n_h=g, scale=dk ** -0.5),
        grid=(batch, ng, nt),
        in_specs=[pl.BlockSpec((tb, gk), lambda b, hh, t: (b * nt + t, hh)),
                  pl.BlockSpec((tb, gk), lambda b, hh, t: (b * nt + t, ng + hh)),
                  pl.BlockSpec((tb, gv), lambda b, hh, t: (b * nt + t, 2 * key // gv + hh)),
                  pl.BlockSpec((tb, gv), lambda b, hh, t: (b * nt + t, hh)),
                  pl.BlockSpec((tb, rank), lambda b, hh, t: (b * nt + t, 0)),
                  pl.BlockSpec((rank, gk), lambda b, hh, t: (0, hh)),
                  pl.BlockSpec((1, gk), lambda b, hh, t: (0, hh)),
                  pl.BlockSpec((1, gv), lambda b, hh, t: (0, hh)),
                  pl.BlockSpec((1, gv), lambda b, hh, t: (0, hh)),
                  pl.BlockSpec((1, gv), lambda b, hh, t: (0, hh))],
        out_specs=pl.BlockSpec((tb, gv), lambda b, hh, t: (b * nt + t, hh)),
        out_shape=jax.ShapeDtypeStruct((batch * seq, val), BF16),
        scratch_shapes=[pltpu.VMEM((g, dv, dk), F32)],
        compiler_params=_params("arbitrary", "arbitrary", "arbitrary"),
        name="gla",
    )(u_qkv, u_qkv, u_qkv, u_r, dg, gate_w2, row(gate_b), row(r_b), row(gn_g), row(gn_b))


def _ffn_up_kernel(a_ref, wg_ref, wu_ref, cw_ref, cb_ref, wd_ref, o_ref, wdbf_ref, carry_ref,
                   wgbf_ref, wubf_ref, *, seq):
    i = pl.program_id(1)
    tm = a_ref.shape[0]

    @pl.when(i == 0)
    def _():
        wgbf_ref[...] = wg_ref[...].astype(BF16)
        wubf_ref[...] = wu_ref[...].astype(BF16)

    wdbf_ref[...] = wd_ref[...].astype(BF16)

    a = a_ref[...]
    gate = jnp.dot(a, wgbf_ref[...], preferred_element_type=F32)
    up = jnp.dot(a, wubf_ref[...], preferred_element_type=F32)
    first = (i * tm) % seq == 0
    tail = carry_ref[...]
    carry_ref[...] = gate[tm - SUBLANES:, :]
    p1 = jnp.where(first, 0.0, tail[SUBLANES - 1:SUBLANES, :])
    p2 = jnp.where(first, 0.0, tail[SUBLANES - 2:SUBLANES - 1, :])
    row = lax.broadcasted_iota(jnp.int32, (tm, 1), 0)
    g1 = jnp.where(row == 0, p1, pltpu.roll(gate, 1, axis=0))
    g2 = jnp.where(row == 0, p2, jnp.where(row == 1, p1, pltpu.roll(gate, 2, axis=0)))
    conv = cw_ref[0:1, :] * g2 + cw_ref[1:2, :] * g1 + cw_ref[2:3, :] * gate + cb_ref[...]
    act = 0.5 * conv * (1.0 + lax.erf(conv * math.sqrt(0.5)))
    o_ref[...] = (act * up).astype(o_ref.dtype)


def _ffn_up(a, w_up3, layer, conv_w3, conv_b2, w_down3, seq):
    m, k = a.shape
    dff = conv_b2.shape[1]
    d_out = w_down3.shape[2]
    tm, tn = FFN_TM, FFN_TN
    nj = dff // tn
    ni = m // tm
    slab = dff // (nj * ni)
    assert m % tm == 0 and dff % tn == 0 and seq % tm == 0 and conv_w3.shape[1] == CONV_W
    assert w_up3.shape[2] == 2 * dff and w_down3.shape[1] == dff
    assert dff % (nj * ni) == 0 and slab % (2 * SUBLANES) == 0
    return pl.pallas_call(
        functools.partial(_ffn_up_kernel, seq=seq),
        grid=(nj, ni),
        in_specs=[pl.BlockSpec((tm, k), lambda j, i: (i, 0)),
                  pl.BlockSpec((None, k, tn), lambda j, i: (layer, 0, j)),
                  pl.BlockSpec((None, k, tn), lambda j, i: (layer, 0, nj + j)),
                  pl.BlockSpec((None, CONV_W, tn), lambda j, i: (layer, 0, j)),
                  pl.BlockSpec((None, 1, tn), lambda j, i: (layer, 0, j)),
                  pl.BlockSpec((None, slab, d_out), lambda j, i: (layer, j * ni + i, 0))],
        out_specs=[pl.BlockSpec((tm, tn), lambda j, i: (i, j)),
                   pl.BlockSpec((slab, d_out), lambda j, i: (j * ni + i, 0))],
        out_shape=[jax.ShapeDtypeStruct((m, dff), BF16),
                   jax.ShapeDtypeStruct((dff, d_out), BF16)],
        scratch_shapes=[pltpu.VMEM((SUBLANES, tn), F32), pltpu.VMEM((k, tn), BF16),
                        pltpu.VMEM((k, tn), BF16)],
        compiler_params=_params("arbitrary", "arbitrary"),
        name="ffn_up",
    )(a, w_up3, w_up3, conv_w3, conv_b2.reshape(conv_b2.shape[0], 1, dff), w_down3)


def _pad_cols(w, n):
    return jnp.pad(w, ((0, 0), (0, n - w.shape[1])))


def _pad_rows(w, n):
    return jnp.pad(w, ((0, n - w.shape[0]), (0, 0)))


def _pad_vec(w, n):
    return jnp.pad(w, (0, n - w.shape[0]))


def kernel(x, even_w_in, even_shift_mu, rw_w0, rw_w2, rw_a0, rw_a2, rw_g2, rw_k_k, rw_k_a, rw_r_k, rw_gn_g, rw_gn_b, even_w_out, odd_w_in, gla_gate_w2, gla_gate_b, gla_r_b, gla_gn_g, gla_gn_b, odd_w_out, ln_mix_g, ln_mix_b, ln_ffn_g, ln_ffn_b, ffn_w_up, ffn_conv_w, ffn_conv_b, ffn_w_down):
    batch, seq, d = x.shape
    depth = ln_mix_g.shape[0]
    alpha = (2 * depth) ** 0.25
    m = batch * seq
    h = x.reshape(m, d)
    h_bf = h.astype(BF16)

    sb_w = SB_HEADS * SB_HEAD_DIM
    rw_w = rw_w0.shape[1]
    lw, la, lg = (_round_up(n, LANES) for n in (RW_DECAY_LORA, RW_AAA_LORA, RW_GATE_LORA))
    tn = MM_TN
    even_wt = jnp.swapaxes(even_w_in, 1, 2)
    odd_wt = jnp.swapaxes(odd_w_in, 1, 2)

    for layer in range(depth):
        i = layer // 2
        if layer % 2 == 0:
            c0 = 3 * sb_w
            c1 = c0 + 3 * rw_w
            c2 = c1 + RW_DECAY_LORA
            c3 = c2 + RW_AAA_LORA
            assert c0 % tn == 0 and c1 % tn == 0
            wt_lora = jnp.concatenate([_pad_rows(even_wt[i, c1:c2], lw),
                                       _pad_rows(even_wt[i, c2:c3], la),
                                       _pad_rows(even_wt[i, c3:], lg)], axis=0)
            n_lora = _round_up(wt_lora.shape[0], LORA_TN)
            wt_lora = _pad_rows(wt_lora, n_lora)[None]
            mu = even_shift_mu[i]
            m0 = 3 * rw_w
            m1 = m0 + RW_DECAY_LORA
            m2 = m1 + RW_AAA_LORA
            mu_lora = _pad_vec(jnp.concatenate([_pad_vec(mu[m0:m1], lw), _pad_vec(mu[m1:m2], la),
                                                _pad_vec(mu[m2:], lg)]), n_lora)
            u_sb = _matmul(h_bf, even_wt, i, 0, c0, BF16, tn, "even_in_sb")
            u_rkv = _matmul(h_bf, even_wt, i, c0 // tn, c1 - c0, F32, tn, "even_in_rkv")
            u_lora, w_out_bf = _matmul_and_cast(h_bf, wt_lora, n_lora, F32, LORA_TN, even_w_out, i,
                                                "even_in_lora")
            o_sb = _sb_attention(u_sb, batch, seq)
            o_rw = _rwkv(u_rkv, u_lora, batch, seq, mu[:m0], mu_lora, rw_w0[i],
                         _pad_rows(rw_w2[i], lw).astype(BF16), rw_a0[i],
                         _pad_rows(rw_a2[i], la).astype(BF16),
                         _pad_rows(rw_g2[i], n_lora - lw - la).astype(BF16),
                         rw_k_k[i], rw_k_a[i], rw_r_k[i].reshape(-1), rw_gn_g[i], rw_gn_b[i])
            h, h_bf = _proj_ln([o_sb, o_rw], w_out_bf, h, ln_mix_g[layer], ln_mix_b[layer],
                               alpha, LN_TM, "even_out_ln")
        else:
            key = gla_gate_b.shape[1]
            val = gla_r_b.shape[1]
            c0 = 2 * key + val
            c1 = c0 + GLA_GATE_RANK
            rank_p = _round_up(GLA_GATE_RANK, LANES)
            assert c0 % tn == 0 and val % tn == 0
            wt_dg = _pad_rows(odd_wt[i, c0:c1], rank_p)[None]
            u_qkv = _matmul(h_bf, odd_wt, i, 0, c0, F32, tn, "odd_in_qkv")
            u_r = _matmul(h_bf, odd_wt[i, c1:][None], 0, 0, val, F32, tn, "odd_in_r")
            dg, w_out_bf = _matmul_and_cast(h_bf, wt_dg, rank_p, F32, rank_p, odd_w_out, i,
                                            "odd_in_gate")
            o = _gla(u_qkv, u_r, dg, batch, seq, _pad_rows(gla_gate_w2[i], rank_p),
                     gla_gate_b[i], gla_r_b[i], gla_gn_g[i], gla_gn_b[i])
            h, h_bf = _proj_ln([o], w_out_bf, h, ln_mix_g[layer], ln_mix_b[layer], alpha, LN_TM,
                               "odd_out_ln")
        mid, w_down_bf = _ffn_up(h_bf, ffn_w_up, layer, ffn_conv_w, ffn_conv_b, ffn_w_down, seq)
        h, h_bf = _proj_ln([mid], w_down_bf, h, ln_ffn_g[layer], ln_ffn_b[layer], alpha, DOWN_TM,
                           "ffn_down_ln")
    return h.reshape(batch, seq, d)
```

```python
import functools
import math

import jax
import jax.numpy as jnp
from jax import lax
from jax.experimental import pallas as pl
from jax.experimental.pallas import tpu as pltpu

F32 = jnp.float32
BF16 = jnp.bfloat16

SB_HEADS = 8
SB_HEAD_DIM = 128
RW_HEAD_DIM = 64
RW_DECAY_LORA = 64
RW_AAA_LORA = 64
RW_GATE_LORA = 160
RW_GN_EPS = 64e-5
GLA_HEADS = 4
GLA_GATE_RANK = 16
GLA_TAU = 16.0
CONV_W = 3
LN_EPS = 1e-5

LANES = 128
SUBLANES = 8
VMEM_LIMIT_BYTES = 56 * 1024 * 1024
MM_TM = 1024
MM_TN = 1024
LORA_TN = 512
LN_TM = 512
FFN_TM = 1024
FFN_TN = 512
DOWN_TM = 256
SB_TQ = 256
SB_HEADS_PER_STEP = 4
SB_LOG_UNDERFLOW = -105.0
RW_CHUNK = 64
RW_PAIR = LANES // RW_HEAD_DIM
RW_BATCH = 4
GLA_CHUNK = 64
GLA_TB = 256
GLA_HEADS_PER_STEP = 4


def _round_up(n, m):
    return (n + m - 1) // m * m


def _params(*sem):
    return pltpu.CompilerParams(dimension_semantics=sem, vmem_limit_bytes=VMEM_LIMIT_BYTES)


def _log_sigmoid(x):
    return jnp.minimum(x, 0.0) - jnp.log(1.0 + jnp.exp(-jnp.abs(x)))


def _sigmoid(x):
    return 1.0 / (1.0 + jnp.exp(-x))


def _dot(a, b):
    return jnp.dot(a.astype(BF16), b.astype(BF16), preferred_element_type=F32)


def _dot_nt(a, b):
    return lax.dot_general(a.astype(BF16), b.astype(BF16), (((1,), (1,)), ((), ())),
                           preferred_element_type=F32)


def _dot_tn(a, b):
    return lax.dot_general(a.astype(BF16), b.astype(BF16), (((0,), (0,)), ((), ())),
                           preferred_element_type=F32)


def _split2(x):
    hi = x.astype(BF16)
    return hi, (x - hi.astype(F32)).astype(BF16)


def _split3(x):
    hi = x.astype(BF16)
    r = x - hi.astype(F32)
    mid = r.astype(BF16)
    return hi, mid, (r - mid.astype(F32)).astype(BF16)


def _dot_hilo(x, w01):
    hi, lo = _split2(x)
    return (jnp.dot(hi, w01, preferred_element_type=F32)
            + jnp.dot(lo, w01, preferred_element_type=F32))


def _cumsum_dot(w01, x):
    return sum(jnp.dot(w01, p, preferred_element_type=F32) for p in _split3(x))


def _dot_split(x, w):
    xh, xl = _split2(x)
    wh, wl = _split2(w)
    return (jnp.dot(xh, wh, preferred_element_type=F32) + jnp.dot(xh, wl, preferred_element_type=F32)
            + jnp.dot(xl, wh, preferred_element_type=F32))


def _layer_norm(y, g, b, eps):
    mu = jnp.mean(y, axis=-1, keepdims=True)
    d = y - mu
    var = jnp.mean(d * d, axis=-1, keepdims=True)
    return d * lax.rsqrt(var + eps) * g + b


def _mm_kernel(a_ref, wt_ref, o_ref, wbf_ref):
    @pl.when(pl.program_id(1) == 0)
    def _():
        wbf_ref[...] = wt_ref[...].astype(BF16)

    o_ref[...] = lax.dot_general(a_ref[...], wbf_ref[...], (((1,), (1,)), ((), ())),
                                 preferred_element_type=F32).astype(o_ref.dtype)


def _matmul(a, wt3, layer, tile0, n_out, out_dtype, tn, name):
    m, k = a.shape
    tm = MM_TM
    assert m % tm == 0 and n_out % tn == 0 and wt3.shape[2] == k
    return pl.pallas_call(
        _mm_kernel,
        grid=(n_out // tn, m // tm),
        in_specs=[pl.BlockSpec((tm, k), lambda j, i: (i, 0)),
                  pl.BlockSpec((None, tn, k), lambda j, i: (layer, tile0 + j, 0))],
        out_specs=pl.BlockSpec((tm, tn), lambda j, i: (i, j)),
        out_shape=jax.ShapeDtypeStruct((m, n_out), out_dtype),
        scratch_shapes=[pltpu.VMEM((tn, k), BF16)],
        compiler_params=_params("arbitrary", "arbitrary"),
        name=name,
    )(a, wt3)


def _mm_cast_kernel(a_ref, wt_ref, side_ref, o_ref, sidebf_ref, *rest):
    if a_ref.dtype == BF16:
        (wbf_ref,) = rest
        a = a_ref[...]
    else:
        abf_ref, wbf_ref = rest
        a = a_ref[...].astype(BF16)
        abf_ref[...] = a

    @pl.when(pl.program_id(1) == 0)
    def _():
        wbf_ref[...] = wt_ref[...].astype(BF16)

    o_ref[...] = lax.dot_general(a, wbf_ref[...], (((1,), (1,)), ((), ())),
                                 preferred_element_type=F32).astype(o_ref.dtype)
    sidebf_ref[...] = side_ref[...].astype(BF16)


def _matmul_and_cast(a, wt3, n_out, out_dtype, tn, side3, side_layer, name):
    m, k = a.shape
    tm = MM_TM
    nj, ni = n_out // tn, m // tm
    rows, d = side3.shape[1], side3.shape[2]
    slab = rows // (nj * ni)
    cast_a = a.dtype != BF16
    assert m % tm == 0 and n_out % tn == 0 and wt3.shape[2] == k and (nj == 1 or not cast_a)
    assert rows % (nj * ni) == 0 and slab % (2 * SUBLANES) == 0
    out_specs = [pl.BlockSpec((tm, tn), lambda j, i: (i, j)),
                 pl.BlockSpec((slab, d), lambda j, i: (j * ni + i, 0))]
    out_shape = [jax.ShapeDtypeStruct((m, n_out), out_dtype),
                 jax.ShapeDtypeStruct((rows, d), BF16)]
    if cast_a:
        out_specs.append(pl.BlockSpec((tm, k), lambda j, i: (i, 0)))
        out_shape.append(jax.ShapeDtypeStruct((m, k), BF16))
    res = pl.pallas_call(
        _mm_cast_kernel,
        grid=(nj, ni),
        in_specs=[pl.BlockSpec((tm, k), lambda j, i: (i, 0)),
                  pl.BlockSpec((None, tn, k), lambda j, i: (0, j, 0)),
                  pl.BlockSpec((None, slab, d), lambda j, i: (side_layer, j * ni + i, 0))],
        out_specs=out_specs,
        out_shape=out_shape,
        scratch_shapes=[pltpu.VMEM((tn, k), BF16)],
        compiler_params=_params("arbitrary", "arbitrary"),
        name=name,
    )(a, wt3, side3)
    return (res[0], res[1], res[2]) if cast_a else (res[0], res[1], a)


def _proj_ln_kernel(*refs, n_parts, alpha):
    a_refs = refs[:n_parts]
    w_ref, res_ref, g_ref, b_ref, o_ref, obf_ref = refs[n_parts:]
    acc = None
    k0 = 0
    for a_ref in a_refs:
        kp = a_ref.shape[1]
        part = jnp.dot(a_ref[...], w_ref[k0:k0 + kp, :], preferred_element_type=F32)
        acc = part if acc is None else acc + part
        k0 += kp
    y = alpha * res_ref[...] + acc
    out = _layer_norm(y, g_ref[...], b_ref[...], LN_EPS)
    o_ref[...] = out
    obf_ref[...] = out.astype(BF16)


def _proj_ln(a_parts, w, res, g, b, alpha, tm, name):
    m, d = res.shape
    k = w.shape[0]
    assert m % tm == 0 and sum(a.shape[1] for a in a_parts) == k and w.dtype == BF16
    tok = lambda s: (s, 0)
    fixed = lambda s: (0, 0)
    in_specs = [pl.BlockSpec((tm, a.shape[1]), tok) for a in a_parts]
    in_specs += [pl.BlockSpec((k, d), fixed, pipeline_mode=pl.Buffered(1)),
                 pl.BlockSpec((tm, d), tok), pl.BlockSpec((1, d), fixed),
                 pl.BlockSpec((1, d), fixed)]
    return pl.pallas_call(
        functools.partial(_proj_ln_kernel, n_parts=len(a_parts), alpha=alpha),
        grid=(m // tm,),
        in_specs=in_specs,
        out_specs=[pl.BlockSpec((tm, d), tok), pl.BlockSpec((tm, d), tok)],
        out_shape=[jax.ShapeDtypeStruct((m, d), F32), jax.ShapeDtypeStruct((m, d), BF16)],
        compiler_params=_params("arbitrary"),
        name=name,
    )(*a_parts, w, res, g.reshape(1, d), b.reshape(1, d))


def _sb_kernel(q_ref, k_ref, v_ref, o_ref, *, tq, dh, scale):
    i = pl.program_id(2)
    n_h = q_ref.shape[1] // dh
    heads = range(n_h)
    cols = [slice(hh * dh, (hh + 1) * dh) for hh in heads]
    row = lax.broadcasted_iota(jnp.int32, (tq, tq), 0)
    col = lax.broadcasted_iota(jnp.int32, (tq, tq), 1)
    after = jnp.where(row > col, 1.0, 0.0).astype(BF16)
    causal = col < row

    def tile(j, state, diagonal):
        k0 = pl.multiple_of(j * tq, tq)
        z = [lax.dot_general(q_ref[:, cols[hh]], k_ref[pl.ds(k0, tq), cols[hh]],
                             (((1,), (1,)), ((), ())), preferred_element_type=F32) * scale
             for hh in heads]
        ls = [_log_sigmoid(z[hh]) for hh in heads]
        lk = [ls[hh] - z[hh] for hh in heads]
        if diagonal:
            lk = [jnp.where(causal, lk[hh], 0.0) for hh in heads]
        suffix = [_dot_hilo(lk[hh], after) + state[hh][0] for hh in heads]
        w = [jnp.exp(ls[hh] + suffix[hh]) for hh in heads]
        if diagonal:
            w = [jnp.where(causal, w[hh], 0.0) for hh in heads]
        acc = [state[hh][1] + jnp.dot(w[hh].astype(BF16), v_ref[pl.ds(k0, tq), cols[hh]],
                                      preferred_element_type=F32) for hh in heads]
        carry = [suffix[hh][:, 0:1] + lk[hh][:, 0:1] for hh in heads]
        return tuple((carry[hh], acc[hh]) for hh in heads)

    init = tuple((jnp.zeros((tq, 1), F32), jnp.zeros((tq, dh), F32)) for _ in heads)
    state = tile(i, init, True)

    def live(st):
        top = st[0][0]
        for hh in heads[1:]:
            top = jnp.maximum(top, st[hh][0])
        return (jnp.max(top) >= SB_LOG_UNDERFLOW).astype(jnp.int32)

    def more(loop):
        return jnp.logical_and(loop[0] < i, loop[1] > 0)

    def step(loop):
        st = tile(i - 1 - loop[0], loop[2], False)
        return loop[0] + 1, live(st), st

    state = lax.while_loop(more, step, (jnp.int32(0), live(state), state))[2]
    for hh in heads:
        o_ref[:, cols[hh]] = state[hh][1].astype(o_ref.dtype)


def _sb_attention(u_sb, batch, seq):
    tq = SB_TQ
    nq = seq // tq
    dh = SB_HEAD_DIM
    g = SB_HEADS_PER_STEP
    ng = SB_HEADS // g
    wd = g * dh
    assert seq % tq == 0 and dh % LANES == 0 and SB_HEADS % g == 0
    return pl.pallas_call(
        functools.partial(_sb_kernel, tq=tq, dh=dh, scale=dh ** -0.5),
        grid=(batch, ng, nq),
        in_specs=[pl.BlockSpec((tq, wd), lambda b, hh, i: (b * nq + i, hh)),
                  pl.BlockSpec((seq, wd), lambda b, hh, i: (b, ng + hh)),
                  pl.BlockSpec((seq, wd), lambda b, hh, i: (b, 2 * ng + hh))],
        out_specs=pl.BlockSpec((tq, wd), lambda b, hh, i: (b * nq + i, hh)),
        out_shape=jax.ShapeDtypeStruct((batch * seq, SB_HEADS * dh), BF16),
        compiler_params=_params("arbitrary", "arbitrary", "arbitrary"),
        name="sb_attention",
    )(u_sb, u_sb, u_sb)


def _token_shift(u, prev_ref, mu):
    rows = u.shape[0]
    row1 = lax.broadcasted_iota(jnp.int32, (rows, 1), 0)
    prev = jnp.where(row1 == 0, prev_ref[SUBLANES - 1:SUBLANES, :], pltpu.roll(u, 1, axis=0))
    prev_ref[...] = u[rows - SUBLANES:, :]
    return u + (prev - u) * mu


def _rwkv_kernel(u_ref, ul_ref, mu_ref, mul_ref, w0_ref, w2_ref, a0_ref, a2_ref, g2_ref, kk_ref,
                 ka_ref, rk_ref, gng_ref, gnb_ref, o_ref, state_ref, prev_ref, prevl_ref,
                 *, chunk, width):
    L, C = chunk, width
    nb = u_ref.shape[0]
    n_pairs = C // LANES
    N = LANES // RW_PAIR
    R = nb * L
    c = pl.program_id(1)

    @pl.when(c == 0)
    def _():
        state_ref[...] = jnp.zeros_like(state_ref)
        prev_ref[...] = jnp.zeros_like(prev_ref)
        prevl_ref[...] = jnp.zeros_like(prevl_ref)

    us = jnp.concatenate([_token_shift(u_ref[bb], prev_ref.at[bb], mu_ref[...])
                          for bb in range(nb)], axis=0)
    ul = jnp.concatenate([_token_shift(ul_ref[bb], prevl_ref.at[bb], mul_ref[...])
                          for bb in range(nb)], axis=0)
    r = us[:, 0:C]
    k = us[:, C:2 * C]
    v = us[:, 2 * C:3 * C]
    o1 = _round_up(RW_DECAY_LORA, LANES)
    o2 = o1 + _round_up(RW_AAA_LORA, LANES)
    dw = ul[:, :o1]
    da = ul[:, o1:o2]
    dg = ul[:, o2:]

    w_log = _log_sigmoid(w0_ref[...] + _dot(jnp.tanh(dw), w2_ref[...])) - 0.5
    log_decay = -jnp.exp(w_log)
    a = _sigmoid(a0_ref[...] + _dot(da, a2_ref[...]))
    gate = _dot(_sigmoid(dg), g2_ref[...])

    rowl = lax.broadcasted_iota(jnp.int32, (R, R), 0)
    coll = lax.broadcasted_iota(jnp.int32, (R, R), 1)
    lower01 = jnp.where(coll <= rowl, jnp.where(coll >= (rowl // L) * L, 1.0, 0.0),
                        0.0).astype(BF16)
    lc = _cumsum_dot(lower01, log_decay)
    g_incl = jnp.exp(lc)
    g_excl = jnp.exp(lc - log_decay)
    g_inv = jnp.exp(-lc)
    g_last = [g_incl[(bb + 1) * L - 1:(bb + 1) * L, :] for bb in range(nb)]
    g_last_rows = jnp.concatenate([jnp.broadcast_to(g_last[bb], (L, C)) for bb in range(nb)],
                                  axis=0)

    lane = lax.broadcasted_iota(jnp.int32, (1, LANES), 1)
    first = lane < N

    def head_sum(x):
        out = []
        for p in range(n_pairs):
            xs = x[:, p * LANES:(p + 1) * LANES]
            s1 = jnp.sum(jnp.where(first, xs, 0.0), axis=-1, keepdims=True)
            s2 = jnp.sum(jnp.where(first, 0.0, xs), axis=-1, keepdims=True)
            out.append(jnp.where(first, s1, s2))
        return jnp.concatenate(out, axis=1)

    kkf = k * kk_ref[...]
    kk = kkf * lax.rsqrt(jnp.maximum(head_sum(kkf * kkf), 1e-24))
    kmod = k * (1.0 + (a - 1.0) * ka_ref[...])
    at_all = -(kk * g_excl)
    bt_all = kk * a * g_inv
    kt_all = kmod * g_inv
    rt_all = r * g_incl
    be_all = bt_all * g_last_rows
    ke_all = kt_all * g_last_rows
    bonus = head_sum(r * kmod * rk_ref[...]) * v

    t_idx = lax.broadcasted_iota(jnp.int32, (L, LANES), 0)
    j_idx = lax.broadcasted_iota(jnp.int32, (L, LANES), 1)
    strict = t_idx > j_idx % N
    incl = t_idx >= j_idx % N
    keep1 = jnp.where(j_idx < N, 1.0, 0.0).astype(BF16)
    keep2 = jnp.where(j_idx < N, 0.0, 1.0).astype(BF16)

    def bd(x):
        xb = x.astype(BF16)
        return jnp.concatenate([xb * keep1, xb * keep2], axis=0)

    units = [(bb, p) for bb in range(nb) for p in range(n_pairs)]
    nu = range(len(units))
    rs = [slice(bb * L, (bb + 1) * L) for bb, _ in units]
    sl = [slice(p * LANES, (p + 1) * LANES) for _, p in units]
    lhs = [jnp.concatenate([at_all[rs[u], sl[u]], rt_all[rs[u], sl[u]]], axis=0) for u in nu]
    s0 = [state_ref[bb, p] for bb, p in units]
    g = [_dot_nt(lhs[u], jnp.concatenate([bd(bt_all[rs[u], sl[u]]), bd(kt_all[rs[u], sl[u]])],
                                         axis=0)) for u in nu]
    ss = [_dot_nt(lhs[u], bd(s0[u])) for u in nu]
    pw = [jnp.where(strict, g[u][:L, :LANES], 0.0) for u in nu]
    a_ak = [jnp.where(strict, g[u][:L, LANES:], 0.0) for u in nu]
    a_rb = [jnp.where(incl, g[u][L:, :LANES], 0.0) for u in nu]
    a_rk = [jnp.where(incl, g[u][L:, LANES:], 0.0) for u in nu]
    av = [_dot(jnp.concatenate([a_ak[u], a_rk[u]], axis=0), bd(v[rs[u], sl[u]])) for u in nu]
    nxt = [_dot(pw[u], bd(pw[u])) for u in nu]
    x = [ss[u][:L] + av[u][:L] for u in nu]
    span = 1
    while True:
        x = [x[u] + _dot(pw[u], bd(x[u])) for u in nu]
        if 2 * span >= L:
            break
        pw = nxt
        span *= 2
        if 2 * span < L:
            nxt = [_dot(pw[u], bd(pw[u])) for u in nu]
    y = [ss[u][L:] + av[u][L:] + _dot(a_rb[u], bd(x[u])) for u in nu]
    for u, (bb, p) in enumerate(units):
        full = _dot_tn(jnp.concatenate([x[u], v[rs[u], sl[u]]], axis=0),
                       jnp.concatenate([be_all[rs[u], sl[u]], ke_all[rs[u], sl[u]]], axis=0))
        state_ref[bb, p] = s0[u] * g_last[bb][:, sl[u]] + jnp.where(first, full[:N], full[N:])

    y = jnp.concatenate([jnp.concatenate(y[bb * n_pairs:(bb + 1) * n_pairs], axis=1)
                         for bb in range(nb)], axis=0)
    d = y - head_sum(y) * (1.0 / N)
    var = head_sum(d * d) * (1.0 / N)
    out = (d * lax.rsqrt(var + RW_GN_EPS) * gng_ref[...] + gnb_ref[...] + bonus) * gate
    for bb in range(nb):
        o_ref[bb] = out[bb * L:(bb + 1) * L, :].astype(o_ref.dtype)


def _rwkv(u_rkv, u_lora, batch, seq, mu, mu_lora, w0, w2, a0, a2, g2, k_k, k_a, r_k, gn_g, gn_b):
    L = RW_CHUNK
    nb = RW_BATCH
    width = w0.shape[0]
    n = RW_HEAD_DIM
    nc = seq // L
    wu = u_rkv.shape[1]
    wl = u_lora.shape[1]
    assert seq % L == 0 and L % SUBLANES == 0 and width % LANES == 0 and L == n
    assert wu == 3 * width and batch % nb == 0
    row = lambda x: x.reshape(1, -1)
    const = lambda shape: pl.BlockSpec(shape, lambda b, c: (0, 0))
    out = pl.pallas_call(
        functools.partial(_rwkv_kernel, chunk=L, width=width),
        grid=(batch // nb, nc),
        in_specs=[pl.BlockSpec((nb, L, wu), lambda b, c: (b, c, 0)),
                  pl.BlockSpec((nb, L, wl), lambda b, c: (b, c, 0)),
                  const((1, wu)), const((1, wl)), const((1, width)), const(w2.shape),
                  const((1, width)), const(a2.shape), const(g2.shape), const((1, width)),
                  const((1, width)), const((1, width)), const((1, width)), const((1, width))],
        out_specs=pl.BlockSpec((nb, L, width), lambda b, c: (b, c, 0)),
        out_shape=jax.ShapeDtypeStruct((batch, seq, width), BF16),
        scratch_shapes=[pltpu.VMEM((nb, width // LANES, n, LANES), F32),
                        pltpu.VMEM((nb, SUBLANES, wu), F32), pltpu.VMEM((nb, SUBLANES, wl), F32)],
        compiler_params=_params("arbitrary", "arbitrary"),
        name="rwkv7",
    )(u_rkv.reshape(batch, seq, wu), u_lora.reshape(batch, seq, wl), row(mu), row(mu_lora),
      row(w0), w2, row(a0), a2, g2, row(k_k), row(k_a), row(r_k), row(gn_g), row(gn_b))
    return out.reshape(batch * seq, width)


def _gla_kernel(q_ref, k_ref, v_ref, r_ref, dg_ref, gw_ref, gb_ref, rb_ref, gng_ref, gnb_ref,
                o_ref, state_ref, *, chunk, n_h, scale):
    L = chunk
    tb = q_ref.shape[0]
    dk = q_ref.shape[1] // n_h
    dv = v_ref.shape[1] // n_h
    n_ch = tb // L
    heads = range(n_h)
    chunks = range(n_ch)
    kc = [slice(hh * dk, (hh + 1) * dk) for hh in heads]
    vc = [slice(hh * dv, (hh + 1) * dv) for hh in heads]
    rc = [slice(c * L, (c + 1) * L) for c in chunks]
    t = pl.program_id(2)

    @pl.when(t == 0)
    def _():
        state_ref[...] = jnp.zeros_like(state_ref)

    row = lax.broadcasted_iota(jnp.int32, (tb, tb), 0)
    col = lax.broadcasted_iota(jnp.int32, (tb, tb), 1)
    chunk_start = (row // L) * L
    lower = jnp.where(col <= row, jnp.where(col >= chunk_start, 1.0, 0.0), 0.0)
    lower01 = lower.astype(BF16)
    in_chunk = lower > 0.0

    log_a = _log_sigmoid(_dot_split(dg_ref[...], gw_ref[...]) + gb_ref[...]) / GLA_TAU
    b = _cumsum_dot(lower01, log_a)
    eb = jnp.exp(b)
    enb = jnp.exp(-b)
    decay = [eb[(c + 1) * L - 1:(c + 1) * L, :] for c in chunks]
    eb_last = jnp.concatenate([jnp.broadcast_to(decay[c], (L, n_h * dk)) for c in chunks], axis=0)
    q_dec = (q_ref[...] * scale * eb).astype(BF16)
    k_inv = k_ref[...] * enb
    k_end = (k_inv * eb_last).astype(BF16)
    k_inv = k_inv.astype(BF16)
    v = v_ref[...].astype(BF16)
    att = [jnp.where(in_chunk, _dot_nt(q_dec[:, kc[hh]], k_inv[:, kc[hh]]), 0.0) for hh in heads]
    o = [_dot(att[hh], v[:, vc[hh]]) for hh in heads]
    kv = [[_dot_tn(v[rc[c], vc[hh]], k_end[rc[c], kc[hh]]) for c in chunks] for hh in heads]
    st = [state_ref[hh] for hh in heads]
    inter = [[] for _ in heads]
    for c in chunks:
        for hh in heads:
            inter[hh].append(_dot_nt(q_dec[rc[c], kc[hh]], st[hh]))
            st[hh] = st[hh] * decay[c][:, kc[hh]] + kv[hh][c]
    for hh in heads:
        state_ref[hh] = st[hh]
        oh = o[hh] + jnp.concatenate(inter[hh], axis=0)
        oh = _layer_norm(oh, gng_ref[:, vc[hh]], gnb_ref[:, vc[hh]], LN_EPS)
        x = r_ref[:, vc[hh]] + rb_ref[:, vc[hh]]
        o_ref[:, vc[hh]] = (oh * (x * _sigmoid(x))).astype(o_ref.dtype)


def _gla(u_qkv, u_r, dg, batch, seq, gate_w2, gate_b, r_b, gn_g, gn_b):
    tb = GLA_TB
    g = GLA_HEADS_PER_STEP
    key = gate_b.shape[0]
    val = r_b.shape[0]
    dk, dv = key // GLA_HEADS, val // GLA_HEADS
    gk, gv = g * dk, g * dv
    ng = GLA_HEADS // g
    nt = seq // tb
    rank = gate_w2.shape[0]
    assert seq % tb == 0 and tb % GLA_CHUNK == 0 and GLA_HEADS % g == 0
    assert (2 * key) % gv == 0 and dg.shape[1] == rank
    row = lambda x: x.reshape(1, -1)
    return pl.pallas_call(
        functools.partial(_gla_kernel, chunk=GLA_CHUNK, n_h=g, scale=dk ** -0.5),
        grid=(batch, ng, nt),
        in_specs=[pl.BlockSpec((tb, gk), lambda b, hh, t: (b * nt + t, hh)),
                  pl.BlockSpec((tb, gk), lambda b, hh, t: (b * nt + t, ng + hh)),
                  pl.BlockSpec((tb, gv), lambda b, hh, t: (b * nt + t, 2 * key // gv + hh)),
                  pl.BlockSpec((tb, gv), lambda b, hh, t: (b * nt + t, hh)),
                  pl.BlockSpec((tb, rank), lambda b, hh, t: (b * nt + t, 0)),
                  pl.BlockSpec((rank, gk), lambda b, hh, t: (0, hh)),
                  pl.BlockSpec((1, gk), lambda b, hh, t: (0, hh)),
                  pl.BlockSpec((1, gv), lambda b, hh, t: (0, hh)),
                  pl.BlockSpec((1, gv), lambda b, hh, t: (0, hh)),
                  pl.BlockSpec((1, gv), lambda b, hh, t: (0, hh))],
        out_specs=pl.BlockSpec((tb, gv), lambda b, hh, t: (b * nt + t, hh)),
        out_shape=jax.ShapeDtypeStruct((batch * seq, val), BF16),
        scratch_shapes=[pltpu.VMEM((g, dv, dk), F32)],
        compiler_params=_params("arbitrary", "arbitrary", "arbitrary"),
        name="gla",
    )(u_qkv, u_qkv, u_qkv, u_r, dg, gate_w2, row(gate_b), row(r_b), row(gn_g), row(gn_b))


def _ffn_up_kernel(a_ref, wg_ref, wu_ref, cw_ref, cb_ref, wd_ref, o_ref, wdbf_ref, carry_ref,
                   wgbf_ref, wubf_ref, *, seq):
    i = pl.program_id(1)
    tm = a_ref.shape[0]

    @pl.when(i == 0)
    def _():
        wgbf_ref[...] = wg_ref[...].astype(BF16)
        wubf_ref[...] = wu_ref[...].astype(BF16)

    wdbf_ref[...] = wd_ref[...].astype(BF16)

    a = a_ref[...]
    gate = jnp.dot(a, wgbf_ref[...], preferred_element_type=F32)
    up = jnp.dot(a, wubf_ref[...], preferred_element_type=F32)
    first = (i * tm) % seq == 0
    tail = carry_ref[...]
    carry_ref[...] = gate[tm - SUBLANES:, :]
    p1 = jnp.where(first, 0.0, tail[SUBLANES - 1:SUBLANES, :])
    p2 = jnp.where(first, 0.0, tail[SUBLANES - 2:SUBLANES - 1, :])
    row = lax.broadcasted_iota(jnp.int32, (tm, 1), 0)
    g1 = jnp.where(row == 0, p1, pltpu.roll(gate, 1, axis=0))
    g2 = jnp.where(row == 0, p2, jnp.where(row == 1, p1, pltpu.roll(gate, 2, axis=0)))
    conv = cw_ref[0:1, :] * g2 + cw_ref[1:2, :] * g1 + cw_ref[2:3, :] * gate + cb_ref[...]
    act = 0.5 * conv * (1.0 + lax.erf(conv * math.sqrt(0.5)))
    o_ref[...] = (act * up).astype(o_ref.dtype)


def _ffn_up(a, w_up3, layer, conv_w3, conv_b2, w_down3, seq):
    m, k = a.shape
    dff = conv_b2.shape[1]
    d_out = w_down3.shape[2]
    tm, tn = FFN_TM, FFN_TN
    nj = dff // tn
    ni = m // tm
    slab = dff // (nj * ni)
    assert m % tm == 0 and dff % tn == 0 and seq % tm == 0 and conv_w3.shape[1] == CONV_W
    assert w_up3.shape[2] == 2 * dff and w_down3.shape[1] == dff
    assert dff % (nj * ni) == 0 and slab % (2 * SUBLANES) == 0
    return pl.pallas_call(
        functools.partial(_ffn_up_kernel, seq=seq),
        grid=(nj, ni),
        in_specs=[pl.BlockSpec((tm, k), lambda j, i: (i, 0)),
                  pl.BlockSpec((None, k, tn), lambda j, i: (layer, 0, j)),
                  pl.BlockSpec((None, k, tn), lambda j, i: (layer, 0, nj + j)),
                  pl.BlockSpec((None, CONV_W, tn), lambda j, i: (layer, 0, j)),
                  pl.BlockSpec((None, 1, tn), lambda j, i: (layer, 0, j)),
                  pl.BlockSpec((None, slab, d_out), lambda j, i: (layer, j * ni + i, 0))],
        out_specs=[pl.BlockSpec((tm, tn), lambda j, i: (i, j)),
                   pl.BlockSpec((slab, d_out), lambda j, i: (j * ni + i, 0))],
        out_shape=[jax.ShapeDtypeStruct((m, dff), BF16),
                   jax.ShapeDtypeStruct((dff, d_out), BF16)],
        scratch_shapes=[pltpu.VMEM((SUBLANES, tn), F32), pltpu.VMEM((k, tn), BF16),
                        pltpu.VMEM((k, tn), BF16)],
        compiler_params=_params("arbitrary", "arbitrary"),
        name="ffn_up",
    )(a, w_up3, w_up3, conv_w3, conv_b2.reshape(conv_b2.shape[0], 1, dff), w_down3)


def _pad_cols(w, n):
    return jnp.pad(w, ((0, 0), (0, n - w.shape[1])))


def _pad_rows(w, n):
    return jnp.pad(w, ((0, n - w.shape[0]), (0, 0)))


def _pad_vec(w, n):
    return jnp.pad(w, (0, n - w.shape[0]))


def kernel(x, even_w_in, even_shift_mu, rw_w0, rw_w2, rw_a0, rw_a2, rw_g2, rw_k_k, rw_k_a, rw_r_k, rw_gn_g, rw_gn_b, even_w_out, odd_w_in, gla_gate_w2, gla_gate_b, gla_r_b, gla_gn_g, gla_gn_b, odd_w_out, ln_mix_g, ln_mix_b, ln_ffn_g, ln_ffn_b, ffn_w_up, ffn_conv_w, ffn_conv_b, ffn_w_down):
    batch, seq, d = x.shape
    depth = ln_mix_g.shape[0]
    alpha = (2 * depth) ** 0.25
    m = batch * seq
    h = x.reshape(m, d)
    h_bf = None

    sb_w = SB_HEADS * SB_HEAD_DIM
    rw_w = rw_w0.shape[1]
    lw, la, lg = (_round_up(n, LANES) for n in (RW_DECAY_LORA, RW_AAA_LORA, RW_GATE_LORA))
    tn = MM_TN
    even_wt = jnp.swapaxes(even_w_in, 1, 2)
    odd_wt = jnp.swapaxes(odd_w_in, 1, 2)

    for layer in range(depth):
        i = layer // 2
        if layer % 2 == 0:
            c0 = 3 * sb_w
            c1 = c0 + 3 * rw_w
            c2 = c1 + RW_DECAY_LORA
            c3 = c2 + RW_AAA_LORA
            assert c0 % tn == 0 and c1 % tn == 0
            wt_lora = jnp.concatenate([_pad_rows(even_wt[i, c1:c2], lw),
                                       _pad_rows(even_wt[i, c2:c3], la),
                                       _pad_rows(even_wt[i, c3:], lg)], axis=0)
            n_lora = _round_up(wt_lora.shape[0], LORA_TN)
            wt_lora = _pad_rows(wt_lora, n_lora)[None]
            mu = even_shift_mu[i]
            m0 = 3 * rw_w
            m1 = m0 + RW_DECAY_LORA
            m2 = m1 + RW_AAA_LORA
            mu_lora = _pad_vec(jnp.concatenate([_pad_vec(mu[m0:m1], lw), _pad_vec(mu[m1:m2], la),
                                                _pad_vec(mu[m2:], lg)]), n_lora)
            u_lora, w_out_bf, h_bf = _matmul_and_cast(h if h_bf is None else h_bf, wt_lora, n_lora,
                                                      F32, LORA_TN, even_w_out, i, "even_in_lora")
            u_sb = _matmul(h_bf, even_wt, i, 0, c0, BF16, tn, "even_in_sb")
            u_rkv = _matmul(h_bf, even_wt, i, c0 // tn, c1 - c0, F32, tn, "even_in_rkv")
            o_sb = _sb_attention(u_sb, batch, seq)
            o_rw = _rwkv(u_rkv, u_lora, batch, seq, mu[:m0], mu_lora, rw_w0[i],
                         _pad_rows(rw_w2[i], lw).astype(BF16), rw_a0[i],
                         _pad_rows(rw_a2[i], la).astype(BF16),
                         _pad_rows(rw_g2[i], n_lora - lw - la).astype(BF16),
                         rw_k_k[i], rw_k_a[i], rw_r_k[i].reshape(-1), rw_gn_g[i], rw_gn_b[i])
            h, h_bf = _proj_ln([o_sb, o_rw], w_out_bf, h, ln_mix_g[layer], ln_mix_b[layer],
                               alpha, LN_TM, "even_out_ln")
        else:
            key = gla_gate_b.shape[1]
            val = gla_r_b.shape[1]
            c0 = 2 * key + val
            c1 = c0 + GLA_GATE_RANK
            rank_p = _round_up(GLA_GATE_RANK, LANES)
            assert c0 % tn == 0 and val % tn == 0
            wt_dg = _pad_rows(odd_wt[i, c0:c1], rank_p)[None]
            u_qkv = _matmul(h_bf, odd_wt, i, 0, c0, F32, tn, "odd_in_qkv")
            u_r = _matmul(h_bf, odd_wt[i, c1:][None], 0, 0, val, F32, tn, "odd_in_r")
            dg, w_out_bf, _ = _matmul_and_cast(h_bf, wt_dg, rank_p, F32, rank_p, odd_w_out, i,
                                               "odd_in_gate")
            o = _gla(u_qkv, u_r, dg, batch, seq, _pad_rows(gla_gate_w2[i], rank_p),
                     gla_gate_b[i], gla_r_b[i], gla_gn_g[i], gla_gn_b[i])
            h, h_bf = _proj_ln([o], w_out_bf, h, ln_mix_g[layer], ln_mix_b[layer], alpha, LN_TM,
                               "odd_out_ln")
        mid, w_down_bf = _ffn_up(h_bf, ffn_w_up, layer, ffn_conv_w, ffn_conv_b, ffn_w_down, seq)
        h, h_bf = _proj_ln([mid], w_down_bf, h, ln_ffn_g[layer], ln_ffn_b[layer], alpha, DOWN_TM,
                           "ffn_down_ln")
    return h.reshape(batch, seq, d)
```

```python
import functools
import math

import jax
import jax.numpy as jnp
from jax import lax
from jax.experimental import pallas as pl
from jax.experimental.pallas import tpu as pltpu

F32 = jnp.float32
BF16 = jnp.bfloat16

SB_HEADS = 8
SB_HEAD_DIM = 128
RW_HEAD_DIM = 64
RW_DECAY_LORA = 64
RW_AAA_LORA = 64
RW_GATE_LORA = 160
RW_GN_EPS = 64e-5
GLA_HEADS = 4
GLA_GATE_RANK = 16
GLA_TAU = 16.0
CONV_W = 3
LN_EPS = 1e-5

LANES = 128
SUBLANES = 8
VMEM_LIMIT_BYTES = 56 * 1024 * 1024
MM_TM = 1024
MM_TN = 1024
LORA_TN = 512
LN_TM = 512
FFN_TM = 1024
FFN_TN = 512
DOWN_TM = 256
SB_TQ = 256
SB_HEADS_PER_STEP = 8
SB_LOG_UNDERFLOW = -105.0
RW_CHUNK = 64
RW_PAIR = LANES // RW_HEAD_DIM
RW_BATCH = 4
GLA_CHUNK = 64
GLA_TB = 256
GLA_HEADS_PER_STEP = 4


def _round_up(n, m):
    return (n + m - 1) // m * m


def _params(*sem):
    return pltpu.CompilerParams(dimension_semantics=sem, vmem_limit_bytes=VMEM_LIMIT_BYTES)


def _log_sigmoid(x):
    return jnp.minimum(x, 0.0) - jnp.log(1.0 + jnp.exp(-jnp.abs(x)))


def _sigmoid(x):
    return 1.0 / (1.0 + jnp.exp(-x))


def _dot(a, b):
    return jnp.dot(a.astype(BF16), b.astype(BF16), preferred_element_type=F32)


def _dot_nt(a, b):
    return lax.dot_general(a.astype(BF16), b.astype(BF16), (((1,), (1,)), ((), ())),
                           preferred_element_type=F32)


def _dot_tn(a, b):
    return lax.dot_general(a.astype(BF16), b.astype(BF16), (((0,), (0,)), ((), ())),
                           preferred_element_type=F32)


def _split2(x):
    hi = x.astype(BF16)
    return hi, (x - hi.astype(F32)).astype(BF16)


def _split3(x):
    hi = x.astype(BF16)
    r = x - hi.astype(F32)
    mid = r.astype(BF16)
    return hi, mid, (r - mid.astype(F32)).astype(BF16)


def _dot_hilo(x, w01):
    hi, lo = _split2(x)
    return (jnp.dot(hi, w01, preferred_element_type=F32)
            + jnp.dot(lo, w01, preferred_element_type=F32))


def _cumsum_dot(w01, x):
    return sum(jnp.dot(w01, p, preferred_element_type=F32) for p in _split3(x))


def _dot_split(x, w):
    xh, xl = _split2(x)
    wh, wl = _split2(w)
    return (jnp.dot(xh, wh, preferred_element_type=F32) + jnp.dot(xh, wl, preferred_element_type=F32)
            + jnp.dot(xl, wh, preferred_element_type=F32))


def _layer_norm(y, g, b, eps):
    mu = jnp.mean(y, axis=-1, keepdims=True)
    d = y - mu
    var = jnp.mean(d * d, axis=-1, keepdims=True)
    return d * lax.rsqrt(var + eps) * g + b


def _mm_kernel(a_ref, wt_ref, o_ref, wbf_ref):
    @pl.when(pl.program_id(1) == 0)
    def _():
        wbf_ref[...] = wt_ref[...].astype(BF16)

    o_ref[...] = lax.dot_general(a_ref[...], wbf_ref[...], (((1,), (1,)), ((), ())),
                                 preferred_element_type=F32).astype(o_ref.dtype)


def _matmul(a, wt3, layer, tile0, n_out, out_dtype, tn, name):
    m, k = a.shape
    tm = MM_TM
    assert m % tm == 0 and n_out % tn == 0 and wt3.shape[2] == k
    return pl.pallas_call(
        _mm_kernel,
        grid=(n_out // tn, m // tm),
        in_specs=[pl.BlockSpec((tm, k), lambda j, i: (i, 0)),
                  pl.BlockSpec((None, tn, k), lambda j, i: (layer, tile0 + j, 0))],
        out_specs=pl.BlockSpec((tm, tn), lambda j, i: (i, j)),
        out_shape=jax.ShapeDtypeStruct((m, n_out), out_dtype),
        scratch_shapes=[pltpu.VMEM((tn, k), BF16)],
        compiler_params=_params("arbitrary", "arbitrary"),
        name=name,
    )(a, wt3)


def _mm_cast_kernel(a_ref, wt_ref, side_ref, o_ref, sidebf_ref, *rest):
    if a_ref.dtype == BF16:
        (wbf_ref,) = rest
        a = a_ref[...]
    else:
        abf_ref, wbf_ref = rest
        a = a_ref[...].astype(BF16)
        abf_ref[...] = a

    @pl.when(pl.program_id(1) == 0)
    def _():
        wbf_ref[...] = wt_ref[...].astype(BF16)

    o_ref[...] = lax.dot_general(a, wbf_ref[...], (((1,), (1,)), ((), ())),
                                 preferred_element_type=F32).astype(o_ref.dtype)
    sidebf_ref[...] = side_ref[...].astype(BF16)


def _matmul_and_cast(a, wt3, n_out, out_dtype, tn, side3, side_layer, name):
    m, k = a.shape
    tm = MM_TM
    nj, ni = n_out // tn, m // tm
    rows, d = side3.shape[1], side3.shape[2]
    slab = rows // (nj * ni)
    cast_a = a.dtype != BF16
    assert m % tm == 0 and n_out % tn == 0 and wt3.shape[2] == k and (nj == 1 or not cast_a)
    assert rows % (nj * ni) == 0 and slab % (2 * SUBLANES) == 0
    out_specs = [pl.BlockSpec((tm, tn), lambda j, i: (i, j)),
                 pl.BlockSpec((slab, d), lambda j, i: (j * ni + i, 0))]
    out_shape = [jax.ShapeDtypeStruct((m, n_out), out_dtype),
                 jax.ShapeDtypeStruct((rows, d), BF16)]
    if cast_a:
        out_specs.append(pl.BlockSpec((tm, k), lambda j, i: (i, 0)))
        out_shape.append(jax.ShapeDtypeStruct((m, k), BF16))
    res = pl.pallas_call(
        _mm_cast_kernel,
        grid=(nj, ni),
        in_specs=[pl.BlockSpec((tm, k), lambda j, i: (i, 0)),
                  pl.BlockSpec((None, tn, k), lambda j, i: (0, j, 0)),
                  pl.BlockSpec((None, slab, d), lambda j, i: (side_layer, j * ni + i, 0))],
        out_specs=out_specs,
        out_shape=out_shape,
        scratch_shapes=[pltpu.VMEM((tn, k), BF16)],
        compiler_params=_params("arbitrary", "arbitrary"),
        name=name,
    )(a, wt3, side3)
    return (res[0], res[1], res[2]) if cast_a else (res[0], res[1], a)


def _proj_ln_kernel(*refs, n_parts, alpha):
    a_refs = refs[:n_parts]
    w_ref, res_ref, g_ref, b_ref, o_ref, obf_ref = refs[n_parts:]
    acc = None
    k0 = 0
    for a_ref in a_refs:
        kp = a_ref.shape[1]
        part = jnp.dot(a_ref[...], w_ref[k0:k0 + kp, :], preferred_element_type=F32)
        acc = part if acc is None else acc + part
        k0 += kp
    y = alpha * res_ref[...] + acc
    out = _layer_norm(y, g_ref[...], b_ref[...], LN_EPS)
    o_ref[...] = out
    obf_ref[...] = out.astype(BF16)


def _proj_ln(a_parts, w, res, g, b, alpha, tm, name):
    m, d = res.shape
    k = w.shape[0]
    assert m % tm == 0 and sum(a.shape[1] for a in a_parts) == k and w.dtype == BF16
    tok = lambda s: (s, 0)
    fixed = lambda s: (0, 0)
    in_specs = [pl.BlockSpec((tm, a.shape[1]), tok) for a in a_parts]
    in_specs += [pl.BlockSpec((k, d), fixed, pipeline_mode=pl.Buffered(1)),
                 pl.BlockSpec((tm, d), tok), pl.BlockSpec((1, d), fixed),
                 pl.BlockSpec((1, d), fixed)]
    return pl.pallas_call(
        functools.partial(_proj_ln_kernel, n_parts=len(a_parts), alpha=alpha),
        grid=(m // tm,),
        in_specs=in_specs,
        out_specs=[pl.BlockSpec((tm, d), tok), pl.BlockSpec((tm, d), tok)],
        out_shape=[jax.ShapeDtypeStruct((m, d), F32), jax.ShapeDtypeStruct((m, d), BF16)],
        compiler_params=_params("arbitrary"),
        name=name,
    )(*a_parts, w, res, g.reshape(1, d), b.reshape(1, d))


def _sb_kernel(q_ref, k_ref, v_ref, o_ref, *, tq, dh, scale):
    i = pl.program_id(2)
    n_h = q_ref.shape[1] // dh
    heads = range(n_h)
    cols = [slice(hh * dh, (hh + 1) * dh) for hh in heads]
    row = lax.broadcasted_iota(jnp.int32, (tq, tq), 0)
    col = lax.broadcasted_iota(jnp.int32, (tq, tq), 1)
    after = jnp.where(row > col, 1.0, 0.0).astype(BF16)
    causal = col < row

    def tile(j, state, diagonal):
        k0 = pl.multiple_of(j * tq, tq)
        z = [lax.dot_general(q_ref[:, cols[hh]], k_ref[pl.ds(k0, tq), cols[hh]],
                             (((1,), (1,)), ((), ())), preferred_element_type=F32) * scale
             for hh in heads]
        ls = [_log_sigmoid(z[hh]) for hh in heads]
        lk = [ls[hh] - z[hh] for hh in heads]
        if diagonal:
            lk = [jnp.where(causal, lk[hh], 0.0) for hh in heads]
        suffix = [_dot_hilo(lk[hh], after) + state[hh][0] for hh in heads]
        w = [jnp.exp(ls[hh] + suffix[hh]) for hh in heads]
        if diagonal:
            w = [jnp.where(causal, w[hh], 0.0) for hh in heads]
        acc = [state[hh][1] + jnp.dot(w[hh].astype(BF16), v_ref[pl.ds(k0, tq), cols[hh]],
                                      preferred_element_type=F32) for hh in heads]
        carry = [suffix[hh][:, 0:1] + lk[hh][:, 0:1] for hh in heads]
        return tuple((carry[hh], acc[hh]) for hh in heads)

    init = tuple((jnp.zeros((tq, 1), F32), jnp.zeros((tq, dh), F32)) for _ in heads)
    state = tile(i, init, True)

    def live(st):
        top = st[0][0]
        for hh in heads[1:]:
            top = jnp.maximum(top, st[hh][0])
        return (jnp.max(top) >= SB_LOG_UNDERFLOW).astype(jnp.int32)

    def more(loop):
        return jnp.logical_and(loop[0] < i, loop[1] > 0)

    def step(loop):
        st = tile(i - 1 - loop[0], loop[2], False)
        return loop[0] + 1, live(st), st

    state = lax.while_loop(more, step, (jnp.int32(0), live(state), state))[2]
    for hh in heads:
        o_ref[:, cols[hh]] = state[hh][1].astype(o_ref.dtype)


def _sb_attention(u_sb, batch, seq):
    tq = SB_TQ
    nq = seq // tq
    dh = SB_HEAD_DIM
    g = SB_HEADS_PER_STEP
    ng = SB_HEADS // g
    wd = g * dh
    assert seq % tq == 0 and dh % LANES == 0 and SB_HEADS % g == 0
    return pl.pallas_call(
        functools.partial(_sb_kernel, tq=tq, dh=dh, scale=dh ** -0.5),
        grid=(batch, ng, nq),
        in_specs=[pl.BlockSpec((tq, wd), lambda b, hh, i: (b * nq + i, hh)),
                  pl.BlockSpec((seq, wd), lambda b, hh, i: (b, ng + hh)),
                  pl.BlockSpec((seq, wd), lambda b, hh, i: (b, 2 * ng + hh))],
        out_specs=pl.BlockSpec((tq, wd), lambda b, hh, i: (b * nq + i, hh)),
        out_shape=jax.ShapeDtypeStruct((batch * seq, SB_HEADS * dh), BF16),
        compiler_params=_params("arbitrary", "arbitrary", "arbitrary"),
        name="sb_attention",
    )(u_sb, u_sb, u_sb)


def _token_shift(u, prev_ref, mu):
    rows = u.shape[0]
    row1 = lax.broadcasted_iota(jnp.int32, (rows, 1), 0)
    prev = jnp.where(row1 == 0, prev_ref[SUBLANES - 1:SUBLANES, :], pltpu.roll(u, 1, axis=0))
    prev_ref[...] = u[rows - SUBLANES:, :]
    return u + (prev - u) * mu


def _rwkv_kernel(u_ref, ul_ref, mu_ref, mul_ref, w0_ref, w2_ref, a0_ref, a2_ref, g2_ref, kk_ref,
                 ka_ref, rk_ref, gng_ref, gnb_ref, o_ref, state_ref, prev_ref, prevl_ref,
                 *, chunk, width):
    L, C = chunk, width
    nb = u_ref.shape[0]
    n_pairs = C // LANES
    N = LANES // RW_PAIR
    R = nb * L
    c = pl.program_id(1)

    @pl.when(c == 0)
    def _():
        state_ref[...] = jnp.zeros_like(state_ref)
        prev_ref[...] = jnp.zeros_like(prev_ref)
        prevl_ref[...] = jnp.zeros_like(prevl_ref)

    us = jnp.concatenate([_token_shift(u_ref[bb], prev_ref.at[bb], mu_ref[...])
                          for bb in range(nb)], axis=0)
    ul = jnp.concatenate([_token_shift(ul_ref[bb], prevl_ref.at[bb], mul_ref[...])
                          for bb in range(nb)], axis=0)
    r = us[:, 0:C]
    k = us[:, C:2 * C]
    v = us[:, 2 * C:3 * C]
    o1 = _round_up(RW_DECAY_LORA, LANES)
    o2 = o1 + _round_up(RW_AAA_LORA, LANES)
    dw = ul[:, :o1]
    da = ul[:, o1:o2]
    dg = ul[:, o2:]

    w_log = _log_sigmoid(w0_ref[...] + _dot(jnp.tanh(dw), w2_ref[...])) - 0.5
    log_decay = -jnp.exp(w_log)
    a = _sigmoid(a0_ref[...] + _dot(da, a2_ref[...]))
    gate = _dot(_sigmoid(dg), g2_ref[...])

    rowl = lax.broadcasted_iota(jnp.int32, (R, R), 0)
    coll = lax.broadcasted_iota(jnp.int32, (R, R), 1)
    lower01 = jnp.where(coll <= rowl, jnp.where(coll >= (rowl // L) * L, 1.0, 0.0),
                        0.0).astype(BF16)
    lc = _cumsum_dot(lower01, log_decay)
    g_incl = jnp.exp(lc)
    g_excl = jnp.exp(lc - log_decay)
    g_inv = jnp.exp(-lc)
    g_last = [g_incl[(bb + 1) * L - 1:(bb + 1) * L, :] for bb in range(nb)]
    g_last_rows = jnp.concatenate([jnp.broadcast_to(g_last[bb], (L, C)) for bb in range(nb)],
                                  axis=0)

    lane = lax.broadcasted_iota(jnp.int32, (1, LANES), 1)
    first = lane < N

    def head_sum(x):
        out = []
        for p in range(n_pairs):
            xs = x[:, p * LANES:(p + 1) * LANES]
            s1 = jnp.sum(jnp.where(first, xs, 0.0), axis=-1, keepdims=True)
            s2 = jnp.sum(jnp.where(first, 0.0, xs), axis=-1, keepdims=True)
            out.append(jnp.where(first, s1, s2))
        return jnp.concatenate(out, axis=1)

    kkf = k * kk_ref[...]
    kk = kkf * lax.rsqrt(jnp.maximum(head_sum(kkf * kkf), 1e-24))
    kmod = k * (1.0 + (a - 1.0) * ka_ref[...])
    at_all = -(kk * g_excl)
    bt_all = kk * a * g_inv
    kt_all = kmod * g_inv
    rt_all = r * g_incl
    be_all = bt_all * g_last_rows
    ke_all = kt_all * g_last_rows
    bonus = head_sum(r * kmod * rk_ref[...]) * v

    t_idx = lax.broadcasted_iota(jnp.int32, (L, LANES), 0)
    j_idx = lax.broadcasted_iota(jnp.int32, (L, LANES), 1)
    strict = t_idx > j_idx % N
    incl = t_idx >= j_idx % N
    keep1 = jnp.where(j_idx < N, 1.0, 0.0).astype(BF16)
    keep2 = jnp.where(j_idx < N, 0.0, 1.0).astype(BF16)

    def bd(x):
        xb = x.astype(BF16)
        return jnp.concatenate([xb * keep1, xb * keep2], axis=0)

    units = [(bb, p) for bb in range(nb) for p in range(n_pairs)]
    nu = range(len(units))
    rs = [slice(bb * L, (bb + 1) * L) for bb, _ in units]
    sl = [slice(p * LANES, (p + 1) * LANES) for _, p in units]
    lhs = [jnp.concatenate([at_all[rs[u], sl[u]], rt_all[rs[u], sl[u]]], axis=0) for u in nu]
    s0 = [state_ref[bb, p] for bb, p in units]
    g = [_dot_nt(lhs[u], jnp.concatenate([bd(bt_all[rs[u], sl[u]]), bd(kt_all[rs[u], sl[u]])],
                                         axis=0)) for u in nu]
    ss = [_dot_nt(lhs[u], bd(s0[u])) for u in nu]
    pw = [jnp.where(strict, g[u][:L, :LANES], 0.0) for u in nu]
    a_ak = [jnp.where(strict, g[u][:L, LANES:], 0.0) for u in nu]
    a_rb = [jnp.where(incl, g[u][L:, :LANES], 0.0) for u in nu]
    a_rk = [jnp.where(incl, g[u][L:, LANES:], 0.0) for u in nu]
    av = [_dot(jnp.concatenate([a_ak[u], a_rk[u]], axis=0), bd(v[rs[u], sl[u]])) for u in nu]
    nxt = [_dot(pw[u], bd(pw[u])) for u in nu]
    x = [ss[u][:L] + av[u][:L] for u in nu]
    span = 1
    while True:
        x = [x[u] + _dot(pw[u], bd(x[u])) for u in nu]
        if 2 * span >= L:
            break
        pw = nxt
        span *= 2
        if 2 * span < L:
            nxt = [_dot(pw[u], bd(pw[u])) for u in nu]
    y = [ss[u][L:] + av[u][L:] + _dot(a_rb[u], bd(x[u])) for u in nu]
    for u, (bb, p) in enumerate(units):
        full = _dot_tn(jnp.concatenate([x[u], v[rs[u], sl[u]]], axis=0),
                       jnp.concatenate([be_all[rs[u], sl[u]], ke_all[rs[u], sl[u]]], axis=0))
        state_ref[bb, p] = s0[u] * g_last[bb][:, sl[u]] + jnp.where(first, full[:N], full[N:])

    y = jnp.concatenate([jnp.concatenate(y[bb * n_pairs:(bb + 1) * n_pairs], axis=1)
                         for bb in range(nb)], axis=0)
    d = y - head_sum(y) * (1.0 / N)
    var = head_sum(d * d) * (1.0 / N)
    out = (d * lax.rsqrt(var + RW_GN_EPS) * gng_ref[...] + gnb_ref[...] + bonus) * gate
    for bb in range(nb):
        o_ref[bb] = out[bb * L:(bb + 1) * L, :].astype(o_ref.dtype)


def _rwkv(u_rkv, u_lora, batch, seq, mu, mu_lora, w0, w2, a0, a2, g2, k_k, k_a, r_k, gn_g, gn_b):
    L = RW_CHUNK
    nb = RW_BATCH
    width = w0.shape[0]
    n = RW_HEAD_DIM
    nc = seq // L
    wu = u_rkv.shape[1]
    wl = u_lora.shape[1]
    assert seq % L == 0 and L % SUBLANES == 0 and width % LANES == 0 and L == n
    assert wu == 3 * width and batch % nb == 0
    row = lambda x: x.reshape(1, -1)
    const = lambda shape: pl.BlockSpec(shape, lambda b, c: (0, 0))
    out = pl.pallas_call(
        functools.partial(_rwkv_kernel, chunk=L, width=width),
        grid=(batch // nb, nc),
        in_specs=[pl.BlockSpec((nb, L, wu), lambda b, c: (b, c, 0)),
                  pl.BlockSpec((nb, L, wl), lambda b, c: (b, c, 0)),
                  const((1, wu)), const((1, wl)), const((1, width)), const(w2.shape),
                  const((1, width)), const(a2.shape), const(g2.shape), const((1, width)),
                  const((1, width)), const((1, width)), const((1, width)), const((1, width))],
        out_specs=pl.BlockSpec((nb, L, width), lambda b, c: (b, c, 0)),
        out_shape=jax.ShapeDtypeStruct((batch, seq, width), BF16),
        scratch_shapes=[pltpu.VMEM((nb, width // LANES, n, LANES), F32),
                        pltpu.VMEM((nb, SUBLANES, wu), F32), pltpu.VMEM((nb, SUBLANES, wl), F32)],
        compiler_params=_params("arbitrary", "arbitrary"),
        name="rwkv7",
    )(u_rkv.reshape(batch, seq, wu), u_lora.reshape(batch, seq, wl), row(mu), row(mu_lora),
      row(w0), w2, row(a0), a2, g2, row(k_k), row(k_a), row(r_k), row(gn_g), row(gn_b))
    return out.reshape(batch * seq, width)


def _gla_kernel(q_ref, k_ref, v_ref, r_ref, dg_ref, gw_ref, gb_ref, rb_ref, gng_ref, gnb_ref,
                o_ref, state_ref, *, chunk, n_h, scale):
    L = chunk
    tb = q_ref.shape[0]
    dk = q_ref.shape[1] // n_h
    dv = v_ref.shape[1] // n_h
    n_ch = tb // L
    heads = range(n_h)
    chunks = range(n_ch)
    kc = [slice(hh * dk, (hh + 1) * dk) for hh in heads]
    vc = [slice(hh * dv, (hh + 1) * dv) for hh in heads]
    rc = [slice(c * L, (c + 1) * L) for c in chunks]
    t = pl.program_id(2)

    @pl.when(t == 0)
    def _():
        state_ref[...] = jnp.zeros_like(state_ref)

    row = lax.broadcasted_iota(jnp.int32, (tb, tb), 0)
    col = lax.broadcasted_iota(jnp.int32, (tb, tb), 1)
    chunk_start = (row // L) * L
    lower = jnp.where(col <= row, jnp.where(col >= chunk_start, 1.0, 0.0), 0.0)
    lower01 = lower.astype(BF16)
    in_chunk = lower > 0.0

    log_a = _log_sigmoid(_dot_split(dg_ref[...], gw_ref[...]) + gb_ref[...]) / GLA_TAU
    b = _cumsum_dot(lower01, log_a)
    eb = jnp.exp(b)
    enb = jnp.exp(-b)
    decay = [eb[(c + 1) * L - 1:(c + 1) * L, :] for c in chunks]
    eb_last = jnp.concatenate([jnp.broadcast_to(decay[c], (L, n_h * dk)) for c in chunks], axis=0)
    q_dec = (q_ref[...] * scale * eb).astype(BF16)
    k_inv = k_ref[...] * enb
    k_end = (k_inv * eb_last).astype(BF16)
    k_inv = k_inv.astype(BF16)
    v = v_ref[...].astype(BF16)
    att = [jnp.where(in_chunk, _dot_nt(q_dec[:, kc[hh]], k_inv[:, kc[hh]]), 0.0) for hh in heads]
    o = [_dot(att[hh], v[:, vc[hh]]) for hh in heads]
    kv = [[_dot_tn(v[rc[c], vc[hh]], k_end[rc[c], kc[hh]]) for c in chunks] for hh in heads]
    st = [state_ref[hh] for hh in heads]
    inter = [[] for _ in heads]
    for c in chunks:
        for hh in heads:
            inter[hh].append(_dot_nt(q_dec[rc[c], kc[hh]], st[hh]))
            st[hh] = st[hh] * decay[c][:, kc[hh]] + kv[hh][c]
    for hh in heads:
        state_ref[hh] = st[hh]
        oh = o[hh] + jnp.concatenate(inter[hh], axis=0)
        oh = _layer_norm(oh, gng_ref[:, vc[hh]], gnb_ref[:, vc[hh]], LN_EPS)
        x = r_ref[:, vc[hh]] + rb_ref[:, vc[hh]]
        o_ref[:, vc[hh]] = (oh * (x * _sigmoid(x))).astype(o_ref.dtype)


def _gla(u_qkv, u_r, dg, batch, seq, gate_w2, gate_b, r_b, gn_g, gn_b):
    tb = GLA_TB
    g = GLA_HEADS_PER_STEP
    key = gate_b.shape[0]
    val = r_b.shape[0]
    dk, dv = key // GLA_HEADS, val // GLA_HEADS
    gk, gv = g * dk, g * dv
    ng = GLA_HEADS // g
    nt = seq // tb
    rank = gate_w2.shape[0]
    assert seq % tb == 0 and tb % GLA_CHUNK == 0 and GLA_HEADS % g == 0
    assert (2 * key) % gv == 0 and dg.shape[1] == rank
    row = lambda x: x.reshape(1, -1)
    return pl.pallas_call(
        functools.partial(_gla_kernel, chunk=GLA_CHUNK, n_h=g, scale=dk ** -0.5),
        grid=(batch, ng, nt),
        in_specs=[pl.BlockSpec((tb, gk), lambda b, hh, t: (b * nt + t, hh)),
                  pl.BlockSpec((tb, gk), lambda b, hh, t: (b * nt + t, ng + hh)),
                  pl.BlockSpec((tb, gv), lambda b, hh, t: (b * nt + t, 2 * key // gv + hh)),
                  pl.BlockSpec((tb, gv), lambda b, hh, t: (b * nt + t, hh)),
                  pl.BlockSpec((tb, rank), lambda b, hh, t: (b * nt + t, 0)),
                  pl.BlockSpec((rank, gk), lambda b, hh, t: (0, hh)),
                  pl.BlockSpec((1, gk), lambda b, hh, t: (0, hh)),
                  pl.BlockSpec((1, gv), lambda b, hh, t: (0, hh)),
                  pl.BlockSpec((1, gv), lambda b, hh, t: (0, hh)),
                  pl.BlockSpec((1, gv), lambda b, hh, t: (0, hh))],
        out_specs=pl.BlockSpec((tb, gv), lambda b, hh, t: (b * nt + t, hh)),
        out_shape=jax.ShapeDtypeStruct((batch * seq, val), BF16),
        scratch_shapes=[pltpu.VMEM((g, dv, dk), F32)],
        compiler_params=_params("arbitrary", "arbitrary", "arbitrary"),
        name="gla",
    )(u_qkv, u_qkv, u_qkv, u_r, dg, gate_w2, row(gate_b), row(r_b), row(gn_g), row(gn_b))


def _ffn_up_kernel(a_ref, wg_ref, wu_ref, cw_ref, cb_ref, wd_ref, o_ref, wdbf_ref, carry_ref,
                   wgbf_ref, wubf_ref, *, seq):
    i = pl.program_id(1)
    tm = a_ref.shape[0]

    @pl.when(i == 0)
    def _():
        wgbf_ref[...] = wg_ref[...].astype(BF16)
        wubf_ref[...] = wu_ref[...].astype(BF16)

    wdbf_ref[...] = wd_ref[...].astype(BF16)

    a = a_ref[...]
    gate = jnp.dot(a, wgbf_ref[...], preferred_element_type=F32)
    up = jnp.dot(a, wubf_ref[...], preferred_element_type=F32)
    first = (i * tm) % seq == 0
    tail = carry_ref[...]
    carry_ref[...] = gate[tm - SUBLANES:, :]
    p1 = jnp.where(first, 0.0, tail[SUBLANES - 1:SUBLANES, :])
    p2 = jnp.where(first, 0.0, tail[SUBLANES - 2:SUBLANES - 1, :])
    row = lax.broadcasted_iota(jnp.int32, (tm, 1), 0)
    g1 = jnp.where(row == 0, p1, pltpu.roll(gate, 1, axis=0))
    g2 = jnp.where(row == 0, p2, jnp.where(row == 1, p1, pltpu.roll(gate, 2, axis=0)))
    conv = cw_ref[0:1, :] * g2 + cw_ref[1:2, :] * g1 + cw_ref[2:3, :] * gate + cb_ref[...]
    act = 0.5 * conv * (1.0 + lax.erf(conv * math.sqrt(0.5)))
    o_ref[...] = (act * up).astype(o_ref.dtype)


def _ffn_up(a, w_up3, layer, conv_w3, conv_b2, w_down3, seq):
    m, k = a.shape
    dff = conv_b2.shape[1]
    d_out = w_down3.shape[2]
    tm, tn = FFN_TM, FFN_TN
    nj = dff // tn
    ni = m // tm
    slab = dff // (nj * ni)
    assert m % tm == 0 and dff % tn == 0 and seq % tm == 0 and conv_w3.shape[1] == CONV_W
    assert w_up3.shape[2] == 2 * dff and w_down3.shape[1] == dff
    assert dff % (nj * ni) == 0 and slab % (2 * SUBLANES) == 0
    return pl.pallas_call(
        functools.partial(_ffn_up_kernel, seq=seq),
        grid=(nj, ni),
        in_specs=[pl.BlockSpec((tm, k), lambda j, i: (i, 0)),
                  pl.BlockSpec((None, k, tn), lambda j, i: (layer, 0, j)),
                  pl.BlockSpec((None, k, tn), lambda j, i: (layer, 0, nj + j)),
                  pl.BlockSpec((None, CONV_W, tn), lambda j, i: (layer, 0, j)),
                  pl.BlockSpec((None, 1, tn), lambda j, i: (layer, 0, j)),
                  pl.BlockSpec((None, slab, d_out), lambda j, i: (layer, j * ni + i, 0))],
        out_specs=[pl.BlockSpec((tm, tn), lambda j, i: (i, j)),
                   pl.BlockSpec((slab, d_out), lambda j, i: (j * ni + i, 0))],
        out_shape=[jax.ShapeDtypeStruct((m, dff), BF16),
                   jax.ShapeDtypeStruct((dff, d_out), BF16)],
        scratch_shapes=[pltpu.VMEM((SUBLANES, tn), F32), pltpu.VMEM((k, tn), BF16),
                        pltpu.VMEM((k, tn), BF16)],
        compiler_params=_params("arbitrary", "arbitrary"),
        name="ffn_up",
    )(a, w_up3, w_up3, conv_w3, conv_b2.reshape(conv_b2.shape[0], 1, dff), w_down3)


def _pad_cols(w, n):
    return jnp.pad(w, ((0, 0), (0, n - w.shape[1])))


def _pad_rows(w, n):
    return jnp.pad(w, ((0, n - w.shape[0]), (0, 0)))


def _pad_vec(w, n):
    return jnp.pad(w, (0, n - w.shape[0]))


def kernel(x, even_w_in, even_shift_mu, rw_w0, rw_w2, rw_a0, rw_a2, rw_g2, rw_k_k, rw_k_a, rw_r_k, rw_gn_g, rw_gn_b, even_w_out, odd_w_in, gla_gate_w2, gla_gate_b, gla_r_b, gla_gn_g, gla_gn_b, odd_w_out, ln_mix_g, ln_mix_b, ln_ffn_g, ln_ffn_b, ffn_w_up, ffn_conv_w, ffn_conv_b, ffn_w_down):
    batch, seq, d = x.shape
    depth = ln_mix_g.shape[0]
    alpha = (2 * depth) ** 0.25
    m = batch * seq
    h = x.reshape(m, d)
    h_bf = None

    sb_w = SB_HEADS * SB_HEAD_DIM
    rw_w = rw_w0.shape[1]
    lw, la, lg = (_round_up(n, LANES) for n in (RW_DECAY_LORA, RW_AAA_LORA, RW_GATE_LORA))
    tn = MM_TN
    even_wt = jnp.swapaxes(even_w_in, 1, 2)
    odd_wt = jnp.swapaxes(odd_w_in, 1, 2)

    for layer in range(depth):
        i = layer // 2
        if layer % 2 == 0:
            c0 = 3 * sb_w
            c1 = c0 + 3 * rw_w
            c2 = c1 + RW_DECAY_LORA
            c3 = c2 + RW_AAA_LORA
            assert c0 % tn == 0 and c1 % tn == 0
            wt_lora = jnp.concatenate([_pad_rows(even_wt[i, c1:c2], lw),
                                       _pad_rows(even_wt[i, c2:c3], la),
                                       _pad_rows(even_wt[i, c3:], lg)], axis=0)
            n_lora = _round_up(wt_lora.shape[0], LORA_TN)
            wt_lora = _pad_rows(wt_lora, n_lora)[None]
            mu = even_shift_mu[i]
            m0 = 3 * rw_w
            m1 = m0 + RW_DECAY_LORA
            m2 = m1 + RW_AAA_LORA
            mu_lora = _pad_vec(jnp.concatenate([_pad_vec(mu[m0:m1], lw), _pad_vec(mu[m1:m2], la),
                                                _pad_vec(mu[m2:], lg)]), n_lora)
            u_lora, w_out_bf, h_bf = _matmul_and_cast(h if h_bf is None else h_bf, wt_lora, n_lora,
                                                      F32, LORA_TN, even_w_out, i, "even_in_lora")
            u_sb = _matmul(h_bf, even_wt, i, 0, c0, BF16, tn, "even_in_sb")
            u_rkv = _matmul(h_bf, even_wt, i, c0 // tn, c1 - c0, F32, tn, "even_in_rkv")
            o_sb = _sb_attention(u_sb, batch, seq)
            o_rw = _rwkv(u_rkv, u_lora, batch, seq, mu[:m0], mu_lora, rw_w0[i],
                         _pad_rows(rw_w2[i], lw).astype(BF16), rw_a0[i],
                         _pad_rows(rw_a2[i], la).astype(BF16),
                         _pad_rows(rw_g2[i], n_lora - lw - la).astype(BF16),
                         rw_k_k[i], rw_k_a[i], rw_r_k[i].reshape(-1), rw_gn_g[i], rw_gn_b[i])
            h, h_bf = _proj_ln([o_sb, o_rw], w_out_bf, h, ln_mix_g[layer], ln_mix_b[layer],
                               alpha, LN_TM, "even_out_ln")
        else:
            key = gla_gate_b.shape[1]
            val = gla_r_b.shape[1]
            c0 = 2 * key + val
            c1 = c0 + GLA_GATE_RANK
            rank_p = _round_up(GLA_GATE_RANK, LANES)
            assert c0 % tn == 0 and val % tn == 0
            wt_dg = _pad_rows(odd_wt[i, c0:c1], rank_p)[None]
            u_qkv = _matmul(h_bf, odd_wt, i, 0, c0, F32, tn, "odd_in_qkv")
            u_r = _matmul(h_bf, odd_wt[i, c1:][None], 0, 0, val, F32, tn, "odd_in_r")
            dg, w_out_bf, _ = _matmul_and_cast(h_bf, wt_dg, rank_p, F32, rank_p, odd_w_out, i,
                                               "odd_in_gate")
            o = _gla(u_qkv, u_r, dg, batch, seq, _pad_rows(gla_gate_w2[i], rank_p),
                     gla_gate_b[i], gla_r_b[i], gla_gn_g[i], gla_gn_b[i])
            h, h_bf = _proj_ln([o], w_out_bf, h, ln_mix_g[layer], ln_mix_b[layer], alpha, LN_TM,
                               "odd_out_ln")
        mid, w_down_bf = _ffn_up(h_bf, ffn_w_up, layer, ffn_conv_w, ffn_conv_b, ffn_w_down, seq)
        h, h_bf = _proj_ln([mid], w_down_bf, h, ln_ffn_g[layer], ln_ffn_b[layer], alpha, DOWN_TM,
                           "ffn_down_ln")
    return h.reshape(batch, seq, d)
```
